```python
import jax, jax.numpy as jnp
from jax import lax
import numpy as np

D_MODEL = 1024
BATCH = 2
SEQ = 8192
DEPTH = 4
DEC_BATCH = 32
DEC_SEQ = 8
PAST_LEN = 8192
PAGE_SIZE = 128

N_HEADS = 12
HEAD_DIM = 64
ATTN_WIDTH = N_HEADS * HEAD_DIM
POOL_WIDTH = D_MODEL - ATTN_WIDTH
MIX_WIDTH = ATTN_WIDTH + POOL_WIDTH
PROJ_WIDTH = 3 * ATTN_WIDTH + POOL_WIDTH
POOL_WINDOWS = (2, 4, 8, 16)
POOL_GROUP = POOL_WIDTH // len(POOL_WINDOWS)
POOL_PREFIX = max(POOL_WINDOWS) - 1
DILATED = ((128, 1), (512, 4), (2048, 16))
MAX_WINDOW = 2048
Q_BLOCK = 128
N_EXPERT_GROUPS = 4
EXPERTS_PER_GROUP = 4
N_EXPERTS = N_EXPERT_GROUPS * EXPERTS_PER_GROUP
TOP_K = 2
D_EXPERT = 256
LN_EPS = 1e-5
NEG_INF = -1e30
ALPHA = (2 * DEPTH) ** 0.25
BETA = (8 * DEPTH) ** -0.25

kernel_name = 'hymba_pool_dilated_hmoe_step'


def layer_norm(x, g, b):
    xf = x.astype(jnp.float32)
    mu = jnp.mean(xf, axis=-1, keepdims=True)
    var = jnp.mean(jnp.square(xf - mu), axis=-1, keepdims=True)
    return ((xf - mu) * lax.rsqrt(var + LN_EPS) * g + b).astype(x.dtype)


def dilated_attention(q, k_all, v_all, q_idx):
    offsets = jnp.asarray(np.stack([np.arange(w // r + 1) * r for w, r in DILATED]), dtype=jnp.int32)
    idx = q_idx[:, None, None] - offsets[None]
    valid = idx >= 0
    idx = jnp.clip(idx, 0, k_all.shape[1] - 1)
    kg = jnp.take(k_all, idx, axis=1)
    vg = jnp.take(v_all, idx, axis=1)
    s = jnp.einsum('bqhd,bqgjhd->bqhgj', q, kg, preferred_element_type=jnp.float32) * (HEAD_DIM ** -0.5)
    s = jnp.where(valid[None, :, None, :, :], s, NEG_INF)
    m = jnp.max(s, axis=(-2, -1), keepdims=True)
    p = jnp.exp(s - m)
    den = jnp.sum(p, axis=-1)
    num = jnp.einsum('bqhgj,bqgjhd->bqhgd', p, vg, preferred_element_type=jnp.float32)
    o_branch = num / den[..., None]
    weight = den / jnp.sum(den, axis=-1, keepdims=True)
    out = jnp.einsum('bqhg,bqhgd->bqhd', weight, o_branch)
    return out.astype(q.dtype)


def attention_seq(q, k, v, k_prefix, v_prefix):
    k_all = jnp.concatenate([k_prefix.astype(k.dtype), k], axis=1)
    v_all = jnp.concatenate([v_prefix.astype(v.dtype), v], axis=1)
    n_prefix = k_prefix.shape[1]
    B, T = q.shape[0], q.shape[1]
    if T <= Q_BLOCK:
        return dilated_attention(q, k_all, v_all, n_prefix + jnp.arange(T, dtype=jnp.int32))
    n_blk = -(-T // Q_BLOCK)
    pad = n_blk * Q_BLOCK - T
    qb = jnp.pad(q, ((0, 0), (0, pad), (0, 0), (0, 0)))
    qb = qb.reshape(B, n_blk, Q_BLOCK, N_HEADS, HEAD_DIM).swapaxes(0, 1)
    ib = (n_prefix + jnp.arange(n_blk * Q_BLOCK, dtype=jnp.int32)).reshape(n_blk, Q_BLOCK)
    ob = lax.map(lambda a: dilated_attention(a[0], k_all, v_all, a[1]), (qb, ib))
    return ob.swapaxes(0, 1).reshape(B, n_blk * Q_BLOCK, N_HEADS, HEAD_DIM)[:, :T]


def pool_mix(u, u_prefix, pos0, w_pool, b_pool, scale):
    B, T, _ = u.shape
    u_ext = jnp.concatenate([u_prefix.astype(u.dtype), u], axis=1)
    csum = jnp.pad(jnp.cumsum(u_ext.astype(jnp.float32), axis=1), ((0, 0), (1, 0), (0, 0)))
    pos = pos0 + jnp.arange(T, dtype=jnp.int32)
    end = POOL_PREFIX + 1
    diffs = []
    for g, w in enumerate(POOL_WINDOWS):
        sl = slice(g * POOL_GROUP, (g + 1) * POOL_GROUP)
        c = csum[:, :, sl]
        window_sum = c[:, end:end + T] - c[:, end - w:end - w + T]
        count = jnp.minimum(pos + 1, w).astype(jnp.float32)[None, :, None]
        diffs.append(window_sum / count - u[:, :, sl].astype(jnp.float32))
    d = jnp.stack(diffs, axis=2)
    y = jnp.einsum('btgc,gce->btge', d, w_pool.astype(jnp.float32)) + b_pool
    y = y.reshape(B, T, POOL_WIDTH) * scale
    return y.astype(u.dtype), u_ext[:, -POOL_PREFIX:]


def hier_moe(x, w_rg, b_rg, w_re, b_re, w_gate, w_up, w_down):
    B, T, D = x.shape
    xt = x.reshape(B * T, D)
    g_logits = jnp.einsum('nd,dg->ng', xt, w_rg, preferred_element_type=jnp.float32) + b_rg
    g_prob = jax.nn.softmax(g_logits, axis=-1)
    g_sel = jnp.argmax(g_logits, axis=-1)
    p_group = jnp.take_along_axis(g_prob, g_sel[:, None], axis=1)
    e_logits = jnp.einsum('nd,gde->nge', xt, w_re, preferred_element_type=jnp.float32) + b_re
    e_sel = jnp.take_along_axis(e_logits, g_sel[:, None, None], axis=1)[:, 0]
    top_v, top_i = lax.top_k(e_sel, TOP_K)
    gates = p_group * jax.nn.softmax(top_v, axis=-1)
    expert_id = g_sel[:, None] * EXPERTS_PER_GROUP + top_i
    combine = jnp.sum(jax.nn.one_hot(expert_id, N_EXPERTS, dtype=jnp.float32) * gates[..., None], axis=1)
    hg = jnp.einsum('nd,edf->nef', xt, w_gate)
    hu = jnp.einsum('nd,edf->nef', xt, w_up)
    h = jax.nn.silu(hg) * hu * combine[..., None].astype(x.dtype)
    y = jnp.einsum('nef,efd->nd', h, w_down)
    return y.reshape(B, T, D)


def trunk_layer(x, k_prefix, v_prefix, u_prefix, pos0, w_in, w_out, w_pool, b_pool, pool_scale,
                ln1_g, ln1_b, w_rg, b_rg, w_re, b_re, w_gate, w_up, w_down, ln2_g, ln2_b):
    B, T, _ = x.shape
    proj = jnp.einsum('btd,de->bte', x, w_in)
    q = proj[..., :ATTN_WIDTH].reshape(B, T, N_HEADS, HEAD_DIM)
    k = proj[..., ATTN_WIDTH:2 * ATTN_WIDTH].reshape(B, T, N_HEADS, HEAD_DIM)
    v = proj[..., 2 * ATTN_WIDTH:3 * ATTN_WIDTH].reshape(B, T, N_HEADS, HEAD_DIM)
    u = proj[..., 3 * ATTN_WIDTH:]
    a = attention_seq(q, k, v, k_prefix, v_prefix).reshape(B, T, ATTN_WIDTH)
    p, pool_tail = pool_mix(u, u_prefix, pos0, w_pool, b_pool, pool_scale)
    mix = jnp.einsum('bte,ed->btd', jnp.concatenate([a, p], axis=-1), w_out)
    h = layer_norm(ALPHA * x + mix, ln1_g, ln1_b)
    y = layer_norm(ALPHA * h + hier_moe(h, w_rg, b_rg, w_re, b_re, w_gate, w_up, w_down), ln2_g, ln2_b)
    return y, k, v, u, pool_tail


def setup_inputs(seed: int = 0) -> dict:
    key = jax.random.key(seed)
    ks = jax.random.split(key, 24)
    f32 = jnp.float32
    kv_rows = min(MAX_WINDOW, PAST_LEN)

    def nrm(k, shape, s):
        return jax.random.normal(k, shape, f32) * s

    return {
        'x_prompt': nrm(ks[0], (BATCH, SEQ, D_MODEL), 1.0),
        'x_sample': nrm(ks[1], (DEC_BATCH, DEC_SEQ, D_MODEL), 1.0),
        'cache_k': nrm(ks[2], (DEPTH, DEC_BATCH, kv_rows, N_HEADS, HEAD_DIM), 1.0),
        'cache_v': nrm(ks[3], (DEPTH, DEC_BATCH, kv_rows, N_HEADS, HEAD_DIM), 1.0),
        'state_pool': nrm(ks[4], (DEPTH, DEC_BATCH, POOL_PREFIX, POOL_WIDTH), 1.0),
        'w_in': nrm(ks[5], (DEPTH, D_MODEL, PROJ_WIDTH), D_MODEL ** -0.5),
        'w_out': nrm(ks[6], (DEPTH, MIX_WIDTH, D_MODEL), MIX_WIDTH ** -0.5 * BETA),
        'w_pool': nrm(ks[7], (DEPTH, len(POOL_WINDOWS), POOL_GROUP, POOL_GROUP), POOL_GROUP ** -0.5),
        'b_pool': nrm(ks[8], (DEPTH, len(POOL_WINDOWS), POOL_GROUP), 0.02),
        'pool_scale': 1.0 + nrm(ks[9], (DEPTH, POOL_WIDTH), 0.1),
        'ln1_g': 1.0 + nrm(ks[10], (DEPTH, D_MODEL), 0.1),
        'ln1_b': nrm(ks[11], (DEPTH, D_MODEL), 0.02),
        'w_rg': nrm(ks[12], (DEPTH, D_MODEL, N_EXPERT_GROUPS), D_MODEL ** -0.5),
        'b_rg': nrm(ks[13], (DEPTH, N_EXPERT_GROUPS), 0.01),
        'w_re': nrm(ks[14], (DEPTH, N_EXPERT_GROUPS, D_MODEL, EXPERTS_PER_GROUP), D_MODEL ** -0.5),
        'b_re': nrm(ks[15], (DEPTH, N_EXPERT_GROUPS, EXPERTS_PER_GROUP), 0.01),
        'w_gate': nrm(ks[16], (DEPTH, N_EXPERTS, D_MODEL, D_EXPERT), D_MODEL ** -0.5),
        'w_up': nrm(ks[17], (DEPTH, N_EXPERTS, D_MODEL, D_EXPERT), D_MODEL ** -0.5),
        'w_down': nrm(ks[18], (DEPTH, N_EXPERTS, D_EXPERT, D_MODEL), D_EXPERT ** -0.5 * BETA),
        'ln2_g': 1.0 + nrm(ks[19], (DEPTH, D_MODEL), 0.1),
        'ln2_b': nrm(ks[20], (DEPTH, D_MODEL), 0.02),
    }


def reference(x_prompt, x_sample, cache_k, cache_v, state_pool, w_in, w_out, w_pool, b_pool,
              pool_scale, ln1_g, ln1_b, w_rg, b_rg, w_re, b_re, w_gate, w_up, w_down, ln2_g, ln2_b):
    xp, xs = x_prompt, x_sample
    B, T, _ = xp.shape
    win_p = min(MAX_WINDOW, T)
    empty_kv = jnp.zeros((B, 0, N_HEADS, HEAD_DIM), xp.dtype)
    zero_pool = jnp.zeros((B, POOL_PREFIX, POOL_WIDTH), xp.dtype)
    pk, pv, pu, sk, sv, su = [], [], [], [], [], []
    for l in range(DEPTH):
        lw = (w_in[l], w_out[l], w_pool[l], b_pool[l], pool_scale[l], ln1_g[l], ln1_b[l],
              w_rg[l], b_rg[l], w_re[l], b_re[l], w_gate[l], w_up[l], w_down[l], ln2_g[l], ln2_b[l])
        xp, k_p, v_p, _, pool_tail_p = trunk_layer(xp, empty_kv, empty_kv, zero_pool, 0, *lw)
        pk.append(k_p[:, -win_p:])
        pv.append(v_p[:, -win_p:])
        pu.append(pool_tail_p)
        xs, k_s, v_s, u_s, _ = trunk_layer(xs, cache_k[l], cache_v[l], state_pool[l], PAST_LEN, *lw)
        sk.append(k_s)
        sv.append(v_s)
        su.append(u_s)
    return (xp, xs, jnp.stack(pk), jnp.stack(pv), jnp.stack(pu), jnp.stack(sk), jnp.stack(sv), jnp.stack(su))
```

```python
import functools

import jax
import jax.numpy as jnp
from jax import lax
from jax.experimental import pallas as pl
from jax.experimental.pallas import tpu as pltpu

N_HEADS = 12
HEAD_DIM = 64
ATTN_WIDTH = N_HEADS * HEAD_DIM
POOL_WIDTH = 256
PROJ_WIDTH = 3 * ATTN_WIDTH + POOL_WIDTH
QKV_WIDTH = 3 * ATTN_WIDTH
POOL_WINDOWS = (2, 4, 8, 16)
POOL_GROUP = POOL_WIDTH // len(POOL_WINDOWS)
POOL_PREFIX = max(POOL_WINDOWS) - 1
POOL_HALO = 16
DILATIONS = (1, 4, 16)
WINDOW_STEPS = 128
MAX_WINDOW = 2048
PAST_LEN = 8192
N_EXPERT_GROUPS = 4
EXPERTS_PER_GROUP = 4
N_EXPERTS = 16
D_EXPERT = 256
LN_EPS = 1e-5
NEG_INF = -1e30
QK_SCALE = HEAD_DIM ** -0.5

LANES = 128
SUBLANES = 8
HEADS_PER_TILE = LANES // HEAD_DIM
N_HEAD_TILES = ATTN_WIDTH // LANES
MXU_WIDTH = 256

ATTN_BLOCK = WINDOW_STEPS
SUPER_BLOCK = ATTN_BLOCK * max(DILATIONS)

_F32 = jnp.float32
_BF16 = jnp.bfloat16
_NT = (((1,), (1,)), ((), ()))


def _params(n_grid_dims, vmem_mib):
    return pltpu.CompilerParams(
        dimension_semantics=("arbitrary",) * n_grid_dims,
        vmem_limit_bytes=vmem_mib * 1024 * 1024)


def _layer_norm(z, g, b):
    mu = jnp.mean(z, axis=-1, keepdims=True)
    zc = z - mu
    var = jnp.mean(zc * zc, axis=-1, keepdims=True)
    return zc * lax.rsqrt(var + LN_EPS) * g + b


def _proj_kernel(x_ref, w_ref, of_ref, ob_ref):
    xb = x_ref[...].astype(_BF16)
    for j in range(0, PROJ_WIDTH, MXU_WIDTH):
        r = jnp.dot(xb, w_ref[:, j:j + MXU_WIDTH], preferred_element_type=_F32)
        of_ref[:, j:j + MXU_WIDTH] = r
        if j < ATTN_WIDTH:
            ob_ref[:, j:j + MXU_WIDTH] = (r * QK_SCALE).astype(_BF16)
        elif j < QKV_WIDTH:
            ob_ref[:, j:j + MXU_WIDTH] = r.astype(_BF16)


def _proj(x, w_in_bf, tm):
    n, d = x.shape
    return pl.pallas_call(
        _proj_kernel,
        grid=(n // tm,),
        in_specs=[pl.BlockSpec((tm, d), lambda i: (i, 0)),
                  pl.BlockSpec((d, PROJ_WIDTH), lambda i: (0, 0))],
        out_specs=[pl.BlockSpec((tm, PROJ_WIDTH), lambda i: (i, 0)),
                   pl.BlockSpec((tm, QKV_WIDTH), lambda i: (i, 0))],
        out_shape=[jax.ShapeDtypeStruct((n, PROJ_WIDTH), _F32),
                   jax.ShapeDtypeStruct((n, QKV_WIDTH), _BF16)],
        compiler_params=_params(1, 48),
        name="proj",
    )(x, w_in_bf)


def _attn_prompt_kernel(q1, q4, q16, k1, k1h, k4, k4h, k16, k16h,
                        v1, v1h, v4, v4h, v16, v16h, out_ref,
                        o_scr, m_scr, l_scr, bias_scr):
    blk = ATTN_BLOCK
    first_super = pl.program_id(2) == 0
    ii = lax.broadcasted_iota(jnp.int32, (blk, 2 * blk), 0)
    jj = lax.broadcasted_iota(jnp.int32, (blk, 2 * blk), 1)
    band = (jj >= ii) & (jj <= ii + WINDOW_STEPS)
    lo = jnp.where(first_super, blk, 0)
    bias_scr[0] = jnp.where(band, 0.0, NEG_INF)
    bias_scr[1] = jnp.where(band & (jj >= lo), 0.0, NEG_INF)
    head0 = lax.broadcasted_iota(jnp.int32, (blk, LANES), 1) < HEAD_DIM

    def block(g, q, k2, v2, bias_idx, rows):
        bias = bias_scr[bias_idx]
        parts = []
        for h in range(HEADS_PER_TILE):
            hm = head0 if h == 0 else jnp.logical_not(head0)
            qh = jnp.where(hm, q, jnp.zeros_like(q))
            s = lax.dot_general(qh, k2, _NT, preferred_element_type=_F32) + bias
            m = jnp.max(s, axis=-1, keepdims=True)
            p = jnp.exp(s - m)
            l = jnp.sum(p, axis=-1, keepdims=True)
            o = jnp.dot(p.astype(_BF16), v2, preferred_element_type=_F32)
            parts.append((m, l, o))
        (m0, l0, o0), (m1, l1, o1) = parts
        m_scr[g, rows, :] = jnp.where(head0, m0, m1)
        l_scr[g, rows, :] = jnp.where(head0, l0, l1)
        o_scr[g, rows, :] = jnp.where(head0, o0, o1)

    block(0, q1[0:blk, :],
          jnp.concatenate([k1h[...], k1[0:blk, :]], axis=0),
          jnp.concatenate([v1h[...], v1[0:blk, :]], axis=0),
          1, pl.ds(0, blk))

    def d1_body(ib, carry):
        off = pl.multiple_of(ib * blk, blk)
        block(0, q1[pl.ds(off, blk), :], k1[pl.ds(off - blk, 2 * blk), :],
              v1[pl.ds(off - blk, 2 * blk), :], 0, pl.ds(off, blk))
        return carry
    lax.fori_loop(1, SUPER_BLOCK // blk, d1_body, 0)

    def dilated(g, r, q_ref, k_ref, kh_ref, v_ref, vh_ref):
        n_blocks = SUPER_BLOCK // (r * blk)

        def body(c, carry):
            for ib in range(n_blocks):
                if ib == 0:
                    k2 = jnp.concatenate([kh_ref[c], k_ref[c, 0:blk, :]], axis=0)
                    v2 = jnp.concatenate([vh_ref[c], v_ref[c, 0:blk, :]], axis=0)
                else:
                    k2 = k_ref[c, (ib - 1) * blk:(ib + 1) * blk, :]
                    v2 = v_ref[c, (ib - 1) * blk:(ib + 1) * blk, :]
                block(g, q_ref[c, ib * blk:(ib + 1) * blk, :], k2, v2,
                      1 if ib == 0 else 0,
                      pl.ds(c + r * blk * ib, blk, stride=r))
            return carry
        lax.fori_loop(0, r, body, 0)

    dilated(1, 4, q4, k4, k4h, v4, v4h)
    dilated(2, 16, q16, k16, k16h, v16, v16h)

    chunk = 256

    def merge(ci, carry):
        rows = pl.ds(pl.multiple_of(ci * chunk, chunk), chunk)
        ms = [m_scr[g, rows, :] for g in range(3)]
        m = jnp.maximum(jnp.maximum(ms[0], ms[1]), ms[2])
        num = jnp.zeros((chunk, LANES), _F32)
        den = jnp.zeros((chunk, LANES), _F32)
        for g in range(3):
            a = jnp.exp(ms[g] - m)
            num = num + a * o_scr[g, rows, :]
            den = den + a * l_scr[g, rows, :]
        out_ref[rows, :] = (num / den).astype(out_ref.dtype)
        return carry
    lax.fori_loop(0, SUPER_BLOCK // chunk, merge, 0)


def _attn_prompt(qkv, qkv4, qkv16, batch, seq):
    blk, sb = ATTN_BLOCK, SUPER_BLOCK
    n_super = seq // sb
    kcol, vcol = N_HEAD_TILES, 2 * N_HEAD_TILES

    def nat(col):
        return pl.BlockSpec((sb, LANES), lambda b, hp, s: (b * n_super + s, col + hp))

    def nat_halo(col):
        per = sb // blk
        return pl.BlockSpec(
            (blk, LANES),
            lambda b, hp, s: (jnp.maximum((b * n_super + s) * per - 1, 0), col + hp))

    def cls(r, col):
        return pl.BlockSpec((r, sb // r, LANES), lambda b, hp, s: (b, s, col + hp))

    def cls_halo(r, col):
        per = sb // (r * blk)
        return pl.BlockSpec((r, blk, LANES),
                            lambda b, hp, s: (b, jnp.maximum(s * per - 1, 0), col + hp))

    in_specs = [nat(0), cls(4, 0), cls(16, 0)]
    operands = [qkv, qkv4, qkv16]
    for col in (kcol, vcol):
        in_specs += [nat(col), nat_halo(col), cls(4, col), cls_halo(4, col),
                     cls(16, col), cls_halo(16, col)]
        operands += [qkv, qkv, qkv4, qkv4, qkv16, qkv16]
    return pl.pallas_call(
        _attn_prompt_kernel,
        grid=(batch, N_HEAD_TILES, n_super),
        in_specs=in_specs,
        out_specs=pl.BlockSpec((sb, LANES), lambda b, hp, s: (b * n_super + s, hp)),
        out_shape=jax.ShapeDtypeStruct((batch * seq, ATTN_WIDTH), _BF16),
        scratch_shapes=[pltpu.VMEM((3, sb, LANES), _F32),
                        pltpu.VMEM((3, sb, LANES), _F32),
                        pltpu.VMEM((3, sb, LANES), _F32),
                        pltpu.VMEM((2, blk, 2 * blk), _F32)],
        compiler_params=_params(3, 48),
        name="attn_prompt",
    )(*operands)


_CACHE_SPLIT = MAX_WINDOW - 512
_PART_A = _CACHE_SPLIT // 2
_PART_B = 512
_PART_B_PAD = 640


def _attn_sample_kernel(proj_ref, ka_ref, kb_ref, va_ref, vb_ref, out_ref, *, n_new):
    rows = HEADS_PER_TILE * n_new
    t_a = lax.broadcasted_iota(jnp.int32, (rows, _PART_A), 0) & (n_new - 1)
    f_a = lax.broadcasted_iota(jnp.int32, (rows, _PART_A), 1)
    w_a = jnp.where((f_a & (n_new - 1)) == t_a, 1.0, 0.0)
    t_b = lax.broadcasted_iota(jnp.int32, (rows, _PART_B_PAD), 0) & (n_new - 1)
    f_b = lax.broadcasted_iota(jnp.int32, (rows, _PART_B_PAD), 1)
    dist = jnp.where(f_b < _PART_B, _PART_B + t_b - f_b, t_b - (f_b - _PART_B))
    reach = ((dist <= 128).astype(_F32)
             + ((dist & 3) == 0).astype(_F32) * (dist <= 512).astype(_F32)
             + ((dist & 15) == 0).astype(_F32))
    w_b = jnp.where((dist >= 0) & (f_b < _PART_B + n_new), reach, 0.0)
    head0 = lax.broadcasted_iota(jnp.int32, (n_new, LANES), 1) < HEAD_DIM
    pad = jnp.zeros((_PART_B_PAD - _PART_B - n_new, LANES), _F32)

    for hp in range(N_HEAD_TILES):
        sl = slice(hp * LANES, (hp + 1) * LANES)
        ksl = slice(ATTN_WIDTH + hp * LANES, ATTN_WIDTH + (hp + 1) * LANES)
        vsl = slice(2 * ATTN_WIDTH + hp * LANES, 2 * ATTN_WIDTH + (hp + 1) * LANES)
        q = proj_ref[:, sl] * QK_SCALE
        q2 = jnp.concatenate([jnp.where(head0, q, 0.0),
                              jnp.where(head0, 0.0, q)], axis=0).astype(_BF16)
        k_a = ka_ref[:, :, sl].reshape(_PART_A, LANES).astype(_BF16)
        v_a = va_ref[:, :, sl].reshape(_PART_A, LANES).astype(_BF16)
        k_b = jnp.concatenate([kb_ref[:, :, sl].reshape(_PART_B, LANES),
                               proj_ref[:, ksl], pad], axis=0).astype(_BF16)
        v_b = jnp.concatenate([vb_ref[:, :, sl].reshape(_PART_B, LANES),
                               proj_ref[:, vsl], pad], axis=0).astype(_BF16)
        s_a = lax.dot_general(q2, k_a, _NT, preferred_element_type=_F32)
        s_b = lax.dot_general(q2, k_b, _NT, preferred_element_type=_F32)
        s_a = jnp.where(w_a > 0.0, s_a, NEG_INF)
        s_b = jnp.where(w_b > 0.0, s_b, NEG_INF)
        m = jnp.maximum(jnp.max(s_a, axis=-1, keepdims=True),
                        jnp.max(s_b, axis=-1, keepdims=True))
        p_a = w_a * jnp.exp(s_a - m)
        p_b = w_b * jnp.exp(s_b - m)
        l = jnp.sum(p_a, axis=-1, keepdims=True) + jnp.sum(p_b, axis=-1, keepdims=True)
        o = (jnp.dot(p_a.astype(_BF16), v_a, preferred_element_type=_F32)
             + jnp.dot(p_b.astype(_BF16), v_b, preferred_element_type=_F32)) / l
        out_ref[:, sl] = jnp.where(head0, o[0:n_new, :], o[n_new:rows, :])


def _attn_sample(proj_f32, cache_k, cache_v, layer, batch, n_new):
    grp = MAX_WINDOW // 16
    a_grp = _CACHE_SPLIT // 16
    b_grp = _PART_B // 16
    spec_a = pl.BlockSpec((None, None, a_grp, n_new, ATTN_WIDTH),
                          lambda b: (layer, b, 0, 0, 0))
    spec_b = pl.BlockSpec((None, None, b_grp, 16, ATTN_WIDTH),
                          lambda b: (layer, b, grp // b_grp - 1, 0, 0))
    return pl.pallas_call(
        functools.partial(_attn_sample_kernel, n_new=n_new),
        grid=(batch,),
        in_specs=[pl.BlockSpec((n_new, PROJ_WIDTH), lambda b: (b, 0)),
                  spec_a, spec_b, spec_a, spec_b],
        out_specs=pl.BlockSpec((n_new, ATTN_WIDTH), lambda b: (b, 0)),
        out_shape=jax.ShapeDtypeStruct((batch * n_new, ATTN_WIDTH), _F32),
        compiler_params=_params(1, 48),
        name="attn_sample",
    )(proj_f32, cache_k, cache_k, cache_v, cache_v)


def _pool_kernel(ext_ref, w_ref, b_ref, sc_ref, out_ref, win_scr, *, bb, seq, chunk, pos0):
    n = bb * chunk
    lane = lax.broadcasted_iota(jnp.int32, (n, POOL_WIDTH), 1)
    window = jnp.where(lane < POOL_GROUP, 2,
                       jnp.where(lane < 2 * POOL_GROUP, 4,
                                 jnp.where(lane < 3 * POOL_GROUP, 8, 16)))
    row = lax.broadcasted_iota(jnp.int32, (bb, chunk, POOL_WIDTH), 1).reshape(n, POOL_WIDTH)

    def body(ci, carry):
        r0 = pl.multiple_of(ci * chunk, SUBLANES)
        win_scr[...] = ext_ref[:, pl.ds(r0, chunk + POOL_HALO), :]

        def shifted(i):
            return win_scr[:, POOL_HALO - i:POOL_HALO - i + chunk, :].reshape(n, POOL_WIDTH)
        cur = shifted(0)
        acc = cur + shifted(1)
        total = acc
        for g, w in enumerate(POOL_WINDOWS[1:], start=1):
            for i in range(w // 2, w):
                acc = acc + shifted(i)
            total = jnp.where(lane >= g * POOL_GROUP, acc, total)
        count = jnp.minimum(pos0 + r0 + row + 1, window).astype(_F32)
        diff = total / count - cur
        y = jnp.dot(diff.astype(_BF16), w_ref[...], preferred_element_type=_F32)
        y = (y + b_ref[...]) * sc_ref[...]
        if bb == 1:
            out_ref[pl.ds(pl.multiple_of(r0, chunk), chunk), :] = y.astype(out_ref.dtype)
        else:
            out_ref[...] = y.astype(out_ref.dtype)
        return carry
    lax.fori_loop(0, seq // chunk, body, 0)


def _pool(ext, w_bd, bias, scale, pos0, bb, chunk):
    batch, rows, _ = ext.shape
    seq = rows - POOL_HALO
    assert bb == 1 or seq == chunk
    return pl.pallas_call(
        functools.partial(_pool_kernel, bb=bb, seq=seq, chunk=chunk, pos0=pos0),
        grid=(batch // bb,),
        in_specs=[pl.BlockSpec((bb, rows, POOL_WIDTH), lambda i: (i, 0, 0)),
                  pl.BlockSpec((POOL_WIDTH, POOL_WIDTH), lambda i: (0, 0)),
                  pl.BlockSpec((1, POOL_WIDTH), lambda i: (0, 0)),
                  pl.BlockSpec((1, POOL_WIDTH), lambda i: (0, 0))],
        out_specs=pl.BlockSpec((bb * seq, POOL_WIDTH), lambda i: (i, 0)),
        out_shape=jax.ShapeDtypeStruct((batch * seq, POOL_WIDTH), _BF16),
        scratch_shapes=[pltpu.VMEM((bb, chunk + POOL_HALO, POOL_WIDTH), _F32)],
        compiler_params=_params(1, 48),
        name="pool",
    )(ext, w_bd, bias, scale)


def _mix_kernel(a_ref, p_ref, x_ref, w_ref, g_ref, b_ref, out_ref, *, alpha):
    mix = (jnp.dot(a_ref[...], w_ref[0:ATTN_WIDTH, :], preferred_element_type=_F32)
           + jnp.dot(p_ref[...], w_ref[ATTN_WIDTH:, :], preferred_element_type=_F32))
    out_ref[...] = _layer_norm(alpha * x_ref[...] + mix, g_ref[...], b_ref[...])


def _mix(a, p, x, w_out_bf, g, b, alpha, tm):
    n, d = x.shape
    return pl.pallas_call(
        functools.partial(_mix_kernel, alpha=alpha),
        grid=(n // tm,),
        in_specs=[pl.BlockSpec((tm, ATTN_WIDTH), lambda i: (i, 0)),
                  pl.BlockSpec((tm, POOL_WIDTH), lambda i: (i, 0)),
                  pl.BlockSpec((tm, d), lambda i: (i, 0)),
                  pl.BlockSpec((ATTN_WIDTH + POOL_WIDTH, d), lambda i: (0, 0)),
                  pl.BlockSpec((1, d), lambda i: (0, 0)),
                  pl.BlockSpec((1, d), lambda i: (0, 0))],
        out_specs=pl.BlockSpec((tm, d), lambda i: (i, 0)),
        out_shape=jax.ShapeDtypeStruct((n, d), _F32),
        compiler_params=_params(1, 48),
        name="mix",
    )(a, p, x, w_out_bf, g, b)


_ROUTER_LANES = LANES
_EXPERT_LANE0 = N_EXPERT_GROUPS


def _router_combine(h, wr_ref, br_ref):
    logits = jnp.dot(h, wr_ref[...], preferred_element_type=_F32,
                     precision=lax.Precision.HIGHEST) + br_ref[...]
    lane = lax.broadcasted_iota(jnp.int32, logits.shape, 1)
    big = _ROUTER_LANES
    g_logit = jnp.where(lane < N_EXPERT_GROUPS, logits, NEG_INF)
    g_max = jnp.max(g_logit, axis=-1, keepdims=True)
    g_sel = jnp.min(jnp.where(g_logit == g_max, lane, big), axis=-1, keepdims=True)
    p_group = 1.0 / jnp.sum(jnp.exp(g_logit - g_max), axis=-1, keepdims=True)
    lane_group = (lane - _EXPERT_LANE0) >> 2
    in_group = ((lane >= _EXPERT_LANE0) & (lane < _EXPERT_LANE0 + N_EXPERTS)
                & (lane_group == g_sel))
    e_logit = jnp.where(in_group, logits, NEG_INF)
    v1 = jnp.max(e_logit, axis=-1, keepdims=True)
    i1 = jnp.min(jnp.where(e_logit == v1, lane, big), axis=-1, keepdims=True)
    e_rest = jnp.where(lane == i1, NEG_INF, e_logit)
    v2 = jnp.max(e_rest, axis=-1, keepdims=True)
    i2 = jnp.min(jnp.where(e_rest == v2, lane, big), axis=-1, keepdims=True)
    t = jnp.exp(v2 - v1)
    w1 = 1.0 / (1.0 + t)
    w2 = t / (1.0 + t)
    return p_group * (jnp.where(lane == i1, w1, 0.0) + jnp.where(lane == i2, w2, 0.0))


def _moe_kernel(h_ref, wr_ref, br_ref, wgu_ref, wd_ref, g_ref, b_ref, out_ref,
                hb_scr, comb_scr, acc_scr, *, alpha):
    e = pl.program_id(1)

    @pl.when(e == 0)
    def _():
        h = h_ref[...]
        hb_scr[...] = h.astype(_BF16)
        comb_scr[...] = _router_combine(h, wr_ref, br_ref)
        acc_scr[...] = jnp.zeros_like(acc_scr)

    comb = comb_scr[...]
    lane = lax.broadcasted_iota(jnp.int32, comb.shape, 1)
    gate = jnp.sum(jnp.where(lane == _EXPERT_LANE0 + e, comb, 0.0), axis=-1, keepdims=True)
    hgu = jnp.dot(hb_scr[...], wgu_ref[...], preferred_element_type=_F32)
    hg = hgu[:, :D_EXPERT]
    hu = hgu[:, D_EXPERT:]
    act = hg * (1.0 / (1.0 + jnp.exp(-hg))) * hu * gate
    acc_scr[...] += jnp.dot(act.astype(_BF16), wd_ref[...], preferred_element_type=_F32)

    @pl.when(e == N_EXPERTS - 1)
    def _():
        out_ref[...] = _layer_norm(alpha * h_ref[...] + acc_scr[...], g_ref[...], b_ref[...])


def _moe(h, w_router, b_router, w_gu_bf, w_down_bf, g, b, alpha, tm):
    n, d = h.shape
    return pl.pallas_call(
        functools.partial(_moe_kernel, alpha=alpha),
        grid=(n // tm, N_EXPERTS),
        in_specs=[pl.BlockSpec((tm, d), lambda i, e: (i, 0)),
                  pl.BlockSpec((d, _ROUTER_LANES), lambda i, e: (0, 0)),
                  pl.BlockSpec((1, _ROUTER_LANES), lambda i, e: (0, 0)),
                  pl.BlockSpec((None, d, 2 * D_EXPERT), lambda i, e: (e, 0, 0)),
                  pl.BlockSpec((None, D_EXPERT, d), lambda i, e: (e, 0, 0)),
                  pl.BlockSpec((1, d), lambda i, e: (0, 0)),
                  pl.BlockSpec((1, d), lambda i, e: (0, 0))],
        out_specs=pl.BlockSpec((tm, d), lambda i, e: (i, 0)),
        out_shape=jax.ShapeDtypeStruct((n, d), _F32),
        scratch_shapes=[pltpu.VMEM((tm, d), _BF16),
                        pltpu.VMEM((tm, _ROUTER_LANES), _F32),
                        pltpu.VMEM((tm, d), _F32)],
        compiler_params=_params(2, 48),
        name="moe",
    )(h, w_router, b_router, w_gu_bf, w_down_bf, g, b)


def _block_diag(w_pool):
    n = len(POOL_WINDOWS)
    eye = jnp.eye(n, dtype=w_pool.dtype)
    return jnp.einsum('gce,gh->gche', w_pool, eye).reshape(POOL_WIDTH, POOL_WIDTH)


def _router_weights(w_rg, b_rg, w_re, b_re):
    d = w_rg.shape[0]
    w_e = jnp.transpose(w_re, (1, 0, 2)).reshape(d, N_EXPERTS)
    w = jnp.concatenate([w_rg, w_e], axis=1)
    w = jnp.pad(w, ((0, 0), (0, _ROUTER_LANES - w.shape[1])))
    bias = jnp.concatenate([b_rg, b_re.reshape(N_EXPERTS)])
    bias = jnp.pad(bias, (0, _ROUTER_LANES - bias.shape[0])).reshape(1, _ROUTER_LANES)
    return w, bias


def _class_major(x, batch, seq, r):
    w = x.shape[-1]
    return x.reshape(batch, seq // r, r, w).transpose(0, 2, 1, 3).reshape(batch * r, seq // r, w)


def _row_tile(n, target):
    return target if n % target == 0 else n


def kernel(x_prompt, x_sample, cache_k, cache_v, state_pool, w_in, w_out, w_pool, b_pool,
           pool_scale, ln1_g, ln1_b, w_rg, b_rg, w_re, b_re, w_gate, w_up, w_down, ln2_g, ln2_b):
    depth = w_in.shape[0]
    alpha = float((2 * depth) ** 0.25)
    bp, tp, d = x_prompt.shape
    bs, ts, _ = x_sample.shape
    kv_rows = cache_k.shape[2]
    assert kv_rows == MAX_WINDOW and tp % SUPER_BLOCK == 0 and ts == SUBLANES
    win_p = min(MAX_WINDOW, tp)

    xp = x_prompt.reshape(bp * tp, d)
    xs = x_sample.reshape(bs * ts, d)
    ck = cache_k.reshape(depth, bs, MAX_WINDOW // 16, 16, ATTN_WIDTH)
    cv = cache_v.reshape(depth, bs, MAX_WINDOW // 16, 16, ATTN_WIDTH)

    pk, pv, pu, sk, sv, su = [], [], [], [], [], []
    for l in range(depth):
        w_in_bf = w_in[l].astype(_BF16)
        w_out_bf = w_out[l].astype(_BF16)
        w_bd = _block_diag(w_pool[l]).astype(_BF16)
        b_pl = b_pool[l].reshape(1, POOL_WIDTH)
        sc_pl = pool_scale[l].reshape(1, POOL_WIDTH)
        w_router, b_router = _router_weights(w_rg[l], b_rg[l], w_re[l], b_re[l])
        w_gu_bf = jnp.concatenate([w_gate[l], w_up[l]], axis=-1).astype(_BF16)
        w_down_bf = w_down[l].astype(_BF16)
        g1, b1 = ln1_g[l].reshape(1, d), ln1_b[l].reshape(1, d)
        g2, b2 = ln2_g[l].reshape(1, d), ln2_b[l].reshape(1, d)

        proj_p, qkv_p = _proj(xp, w_in_bf, _row_tile(bp * tp, 512))
        a_p = _attn_prompt(qkv_p, _class_major(qkv_p, bp, tp, 4),
                           _class_major(qkv_p, bp, tp, 16), bp, tp)
        proj_p3 = proj_p.reshape(bp, tp, PROJ_WIDTH)
        u_p = proj_p3[:, :, QKV_WIDTH:]
        ext_p = jnp.concatenate([jnp.zeros((bp, POOL_HALO, POOL_WIDTH), _F32), u_p], axis=1)
        p_p = _pool(ext_p, w_bd, b_pl, sc_pl, 0, 1, 128)
        h_p = _mix(a_p, p_p, xp, w_out_bf, g1, b1, alpha, _row_tile(bp * tp, 512))
        xp = _moe(h_p, w_router, b_router, w_gu_bf, w_down_bf, g2, b2, alpha,
                  _row_tile(bp * tp, 1024))
        pk.append(proj_p3[:, tp - win_p:, ATTN_WIDTH:2 * ATTN_WIDTH]
                  .reshape(bp, win_p, N_HEADS, HEAD_DIM))
        pv.append(proj_p3[:, tp - win_p:, 2 * ATTN_WIDTH:QKV_WIDTH]
                  .reshape(bp, win_p, N_HEADS, HEAD_DIM))
        pu.append(u_p[:, tp - POOL_PREFIX:])

        proj_s, _ = _proj(xs, w_in_bf, bs * ts)
        a_s = _attn_sample(proj_s, ck, cv, l, bs, ts)
        proj_s3 = proj_s.reshape(bs, ts, PROJ_WIDTH)
        u_s = proj_s3[:, :, QKV_WIDTH:]
        ext_s = jnp.concatenate(
            [jnp.zeros((bs, POOL_HALO - POOL_PREFIX, POOL_WIDTH), _F32), state_pool[l], u_s],
            axis=1)
        p_s = _pool(ext_s, w_bd, b_pl, sc_pl, PAST_LEN, bs, ts)
        h_s = _mix(a_s.astype(_BF16), p_s, xs, w_out_bf, g1, b1, alpha, bs * ts)
        xs = _moe(h_s, w_router, b_router, w_gu_bf, w_down_bf, g2, b2, alpha, bs * ts)
        sk.append(proj_s3[:, :, ATTN_WIDTH:2 * ATTN_WIDTH].reshape(bs, ts, N_HEADS, HEAD_DIM))
        sv.append(proj_s3[:, :, 2 * ATTN_WIDTH:QKV_WIDTH].reshape(bs, ts, N_HEADS, HEAD_DIM))
        su.append(u_s)

    return (xp.reshape(bp, tp, d), xs.reshape(bs, ts, d), jnp.stack(pk), jnp.stack(pv),
            jnp.stack(pu), jnp.stack(sk), jnp.stack(sv), jnp.stack(su))
```

```python
import functools
import math

import jax
import jax.numpy as jnp
from jax import lax
from jax.experimental import pallas as pl
from jax.experimental.pallas import tpu as pltpu

N_HEADS = 12
HEAD_DIM = 64
ATTN_WIDTH = N_HEADS * HEAD_DIM
POOL_WIDTH = 256
PROJ_WIDTH = 3 * ATTN_WIDTH + POOL_WIDTH
QKV_WIDTH = 3 * ATTN_WIDTH
POOL_WINDOWS = (2, 4, 8, 16)
POOL_GROUP = POOL_WIDTH // len(POOL_WINDOWS)
POOL_PREFIX = max(POOL_WINDOWS) - 1
POOL_HALO = 16
DILATIONS = (1, 4, 16)
WINDOW_STEPS = 128
MAX_WINDOW = 2048
PAST_LEN = 8192
N_EXPERT_GROUPS = 4
EXPERTS_PER_GROUP = 4
N_EXPERTS = 16
D_EXPERT = 256
LN_EPS = 1e-5
NEG_INF = -1e30
QK_SCALE = HEAD_DIM ** -0.5
LOG2_E = math.log2(math.e)

LANES = 128
SUBLANES = 8
HEADS_PER_TILE = LANES // HEAD_DIM
N_HEAD_TILES = ATTN_WIDTH // LANES
MXU_WIDTH = 256

ATTN_BLOCK = WINDOW_STEPS
SUPER_BLOCK = ATTN_BLOCK * max(DILATIONS)
BLOCKS_PER_SUPER = SUPER_BLOCK // ATTN_BLOCK
ATTN_UNROLL = 16

_F32 = jnp.float32
_BF16 = jnp.bfloat16
_NT = (((1,), (1,)), ((), ()))


def _params(n_grid_dims, vmem_mib):
    return pltpu.CompilerParams(
        dimension_semantics=("arbitrary",) * n_grid_dims,
        vmem_limit_bytes=vmem_mib * 1024 * 1024)


def _layer_norm(z, g, b):
    mu = jnp.mean(z, axis=-1, keepdims=True)
    zc = z - mu
    var = jnp.mean(zc * zc, axis=-1, keepdims=True)
    return zc * lax.rsqrt(var + LN_EPS) * g + b


def _proj_kernel(x_ref, w_ref, of_ref, ob_ref):
    xb = x_ref[...].astype(_BF16)
    for j in range(0, PROJ_WIDTH, MXU_WIDTH):
        r = jnp.dot(xb, w_ref[:, j:j + MXU_WIDTH], preferred_element_type=_F32)
        of_ref[:, j:j + MXU_WIDTH] = r
        if j < ATTN_WIDTH:
            ob_ref[:, j:j + MXU_WIDTH] = (r * (QK_SCALE * LOG2_E)).astype(_BF16)
        elif j < QKV_WIDTH:
            ob_ref[:, j:j + MXU_WIDTH] = r.astype(_BF16)


def _proj(x, w_in_bf, tm):
    n, d = x.shape
    return pl.pallas_call(
        _proj_kernel,
        grid=(n // tm,),
        in_specs=[pl.BlockSpec((tm, d), lambda i: (i, 0)),
                  pl.BlockSpec((d, PROJ_WIDTH), lambda i: (0, 0))],
        out_specs=[pl.BlockSpec((tm, PROJ_WIDTH), lambda i: (i, 0)),
                   pl.BlockSpec((tm, QKV_WIDTH), lambda i: (i, 0))],
        out_shape=[jax.ShapeDtypeStruct((n, PROJ_WIDTH), _F32),
                   jax.ShapeDtypeStruct((n, QKV_WIDTH), _BF16)],
        compiler_params=_params(1, 48),
        name="proj",
    )(x, w_in_bf)


def _attn_prompt_kernel(q1, q4, q16, k1, k1h, k4, k4h, k16, k16h,
                        v1, v1h, v4, v4h, v16, v16h, out_ref,
                        kc1, kc4, kc16, vc1, vc4, vc16,
                        o_scr, m_scr, l_scr, bias_scr):
    blk = ATTN_BLOCK
    first_super = pl.program_id(2) == 0
    ii = lax.broadcasted_iota(jnp.int32, (blk, 2 * blk), 0)
    jj = lax.broadcasted_iota(jnp.int32, (blk, 2 * blk), 1)
    band = (jj >= ii) & (jj <= ii + WINDOW_STEPS)
    lo = jnp.where(first_super, blk, 0)
    bias_scr[0] = jnp.where(band, 0.0, NEG_INF)
    bias_scr[1] = jnp.where(band & (jj >= lo), 0.0, NEG_INF)
    head0 = lax.broadcasted_iota(jnp.int32, (blk, LANES), 1) < HEAD_DIM

    for halo, main, cat in ((k1h, k1, kc1), (k4h, k4, kc4), (k16h, k16, kc16),
                            (v1h, v1, vc1), (v4h, v4, vc4), (v16h, v16, vc16)):
        cat[:, 0:blk, :] = halo[...]
        cat[:, blk:, :] = main[...]

    def block(g, r, q_ref, kc, vc, c, ib):
        if isinstance(ib, int):
            off = ib * blk
            bias = bias_scr[1 if ib == 0 else 0]
        else:
            off = pl.multiple_of(ib * blk, blk)
            bias = bias_scr[jnp.where(ib == 0, 1, 0)]
        q = q_ref[c, pl.ds(off, blk), :]
        k2 = kc[c, pl.ds(off, 2 * blk), :]
        v2 = vc[c, pl.ds(off, 2 * blk), :]
        zero = jnp.zeros_like(q)
        qq = jnp.concatenate([jnp.where(head0, q, zero), jnp.where(head0, zero, q)], axis=0)
        s = lax.dot_general(qq, k2, _NT, preferred_element_type=_F32)
        ms, ls, ps = [], [], []
        for h in range(HEADS_PER_TILE):
            sh = s[h * blk:(h + 1) * blk, :] + bias
            m = jnp.max(sh, axis=-1, keepdims=True)
            p = jnp.exp2(sh - m)
            ms.append(m)
            ls.append(jnp.sum(p, axis=-1, keepdims=True))
            ps.append(p.astype(_BF16))
        o = jnp.dot(jnp.concatenate(ps, axis=0), v2, preferred_element_type=_F32)
        if r == 1:
            rows = pl.ds(off, blk)
        else:
            rows = pl.ds(c + r * off, blk, stride=r)
        m_scr[g, rows, :] = jnp.where(head0, ms[0], ms[1])
        l_scr[g, rows, :] = jnp.where(head0, ls[0], ls[1])
        o_scr[g, rows, :] = jnp.where(head0, o[0:blk, :], o[blk:2 * blk, :])

    def branch(g, r, q_ref, kc, vc):
        per_class = BLOCKS_PER_SUPER // r
        assert r == 1 or ATTN_UNROLL % per_class == 0

        def body(j, carry):
            for u in range(ATTN_UNROLL):
                if r == 1:
                    c, ib = 0, j * ATTN_UNROLL + u
                else:
                    c = j * (ATTN_UNROLL // per_class) + u // per_class
                    ib = u % per_class
                block(g, r, q_ref, kc, vc, c, ib)
            return carry
        lax.fori_loop(0, BLOCKS_PER_SUPER // ATTN_UNROLL, body, 0)

    branch(0, 1, q1, kc1, vc1)
    branch(1, 4, q4, kc4, vc4)
    branch(2, 16, q16, kc16, vc16)

    chunk = 256

    def merge(ci, carry):
        rows = pl.ds(pl.multiple_of(ci * chunk, chunk), chunk)
        ms = [m_scr[g, rows, :] for g in range(3)]
        m = jnp.maximum(jnp.maximum(ms[0], ms[1]), ms[2])
        num = jnp.zeros((chunk, LANES), _F32)
        den = jnp.zeros((chunk, LANES), _F32)
        for g in range(3):
            a = jnp.exp2(ms[g] - m)
            num = num + a * o_scr[g, rows, :]
            den = den + a * l_scr[g, rows, :]
        out_ref[rows, :] = (num / den).astype(out_ref.dtype)
        return carry
    lax.fori_loop(0, SUPER_BLOCK // chunk, merge, 0)


def _attn_prompt(qkv, qkv4, qkv16, batch, seq):
    blk, sb = ATTN_BLOCK, SUPER_BLOCK
    n_super = seq // sb
    kcol, vcol = N_HEAD_TILES, 2 * N_HEAD_TILES
    views = {1: qkv.reshape(batch, seq, QKV_WIDTH), 4: qkv4, 16: qkv16}

    def main(r, col):
        return pl.BlockSpec((r, sb // r, LANES), lambda b, hp, s: (b, s, col + hp))

    def halo(r, col):
        per = sb // (r * blk)
        return pl.BlockSpec((r, blk, LANES),
                            lambda b, hp, s: (b, jnp.maximum(s * per - 1, 0), col + hp))

    in_specs = [main(r, 0) for r in DILATIONS]
    operands = [views[r] for r in DILATIONS]
    for col in (kcol, vcol):
        for r in DILATIONS:
            in_specs += [main(r, col), halo(r, col)]
            operands += [views[r], views[r]]
    cat_shapes = [pltpu.VMEM((r, sb // r + blk, LANES), _BF16) for r in DILATIONS]
    return pl.pallas_call(
        _attn_prompt_kernel,
        grid=(batch, N_HEAD_TILES, n_super),
        in_specs=in_specs,
        out_specs=pl.BlockSpec((sb, LANES), lambda b, hp, s: (b * n_super + s, hp)),
        out_shape=jax.ShapeDtypeStruct((batch * seq, ATTN_WIDTH), _BF16),
        scratch_shapes=cat_shapes + cat_shapes + [
            pltpu.VMEM((3, sb, LANES), _F32),
            pltpu.VMEM((3, sb, LANES), _F32),
            pltpu.VMEM((3, sb, LANES), _F32),
            pltpu.VMEM((2, blk, 2 * blk), _F32)],
        compiler_params=_params(3, 48),
        name="attn_prompt",
    )(*operands)


def _branch_count(dist):
    return ((dist <= 128).astype(_F32)
            + (((dist & 3) == 0) & (dist <= 512)).astype(_F32)
            + (((dist & 15) == 0) & (dist <= MAX_WINDOW)).astype(_F32))


def _attn_sample_kernel(proj_ref, kt_ref, vt_ref, out_ref, *, n_new):
    rows = HEADS_PER_TILE * n_new
    t_c = lax.broadcasted_iota(jnp.int32, (rows, MAX_WINDOW), 0) & (n_new - 1)
    f_c = lax.broadcasted_iota(jnp.int32, (rows, MAX_WINDOW), 1)
    w_c = _branch_count(MAX_WINDOW + t_c - f_c)
    t_n = lax.broadcasted_iota(jnp.int32, (rows, LANES), 0) & (n_new - 1)
    f_n = lax.broadcasted_iota(jnp.int32, (rows, LANES), 1)
    d_n = t_n - f_n
    w_n = jnp.where((d_n >= 0) & (f_n < n_new), _branch_count(d_n), 0.0)
    head0 = lax.broadcasted_iota(jnp.int32, (n_new, LANES), 1) < HEAD_DIM
    pad = jnp.zeros((LANES - n_new, LANES), _F32)

    for hp in range(N_HEAD_TILES):
        sl = slice(hp * LANES, (hp + 1) * LANES)
        ksl = slice(ATTN_WIDTH + hp * LANES, ATTN_WIDTH + (hp + 1) * LANES)
        vsl = slice(2 * ATTN_WIDTH + hp * LANES, 2 * ATTN_WIDTH + (hp + 1) * LANES)
        heads = slice(hp * HEADS_PER_TILE, (hp + 1) * HEADS_PER_TILE)
        q = proj_ref[:, sl] * QK_SCALE
        q2 = jnp.concatenate([jnp.where(head0, q, 0.0),
                              jnp.where(head0, 0.0, q)], axis=0).astype(_BF16)
        kt = kt_ref[heads].reshape(LANES, MAX_WINDOW).astype(_BF16)
        vt = vt_ref[heads].reshape(LANES, MAX_WINDOW).astype(_BF16)
        k_n = jnp.concatenate([proj_ref[:, ksl], pad], axis=0).astype(_BF16)
        v_n = jnp.concatenate([proj_ref[:, vsl], pad], axis=0).astype(_BF16)
        s_c = jnp.dot(q2, kt, preferred_element_type=_F32)
        s_n = lax.dot_general(q2, k_n, _NT, preferred_element_type=_F32)
        s_c = jnp.where(w_c > 0.0, s_c, NEG_INF)
        s_n = jnp.where(w_n > 0.0, s_n, NEG_INF)
        m = jnp.maximum(jnp.max(s_c, axis=-1, keepdims=True),
                        jnp.max(s_n, axis=-1, keepdims=True))
        p_c = w_c * jnp.exp(s_c - m)
        p_n = w_n * jnp.exp(s_n - m)
        l = jnp.sum(p_c, axis=-1, keepdims=True) + jnp.sum(p_n, axis=-1, keepdims=True)
        o = (lax.dot_general(p_c.astype(_BF16), vt, _NT, preferred_element_type=_F32)
             + jnp.dot(p_n.astype(_BF16), v_n, preferred_element_type=_F32)) / l
        out_ref[:, sl] = jnp.where(head0, o[0:n_new, :], o[n_new:rows, :])


def _attn_sample(proj_f32, cache_kt, cache_vt, layer, batch, n_new):
    spec = pl.BlockSpec((None, None, N_HEADS, HEAD_DIM, MAX_WINDOW),
                        lambda b: (layer, b, 0, 0, 0))
    return pl.pallas_call(
        functools.partial(_attn_sample_kernel, n_new=n_new),
        grid=(batch,),
        in_specs=[pl.BlockSpec((n_new, PROJ_WIDTH), lambda b: (b, 0)), spec, spec],
        out_specs=pl.BlockSpec((n_new, ATTN_WIDTH), lambda b: (b, 0)),
        out_shape=jax.ShapeDtypeStruct((batch * n_new, ATTN_WIDTH), _F32),
        compiler_params=_params(1, 48),
        name="attn_sample",
    )(proj_f32, cache_kt, cache_vt)


def _pool_kernel(ext_ref, w_ref, b_ref, sc_ref, out_ref, win_scr, *, bb, seq, chunk, pos0):
    n = bb * chunk
    lane = lax.broadcasted_iota(jnp.int32, (n, POOL_WIDTH), 1)
    window = jnp.where(lane < POOL_GROUP, 2,
                       jnp.where(lane < 2 * POOL_GROUP, 4,
                                 jnp.where(lane < 3 * POOL_GROUP, 8, 16)))
    row = lax.broadcasted_iota(jnp.int32, (bb, chunk, POOL_WIDTH), 1).reshape(n, POOL_WIDTH)

    def body(ci, carry):
        r0 = pl.multiple_of(ci * chunk, SUBLANES)
        win_scr[...] = ext_ref[:, pl.ds(r0, chunk + POOL_HALO), :]

        def shifted(i):
            return win_scr[:, POOL_HALO - i:POOL_HALO - i + chunk, :].reshape(n, POOL_WIDTH)
        cur = shifted(0)
        acc = cur + shifted(1)
        total = acc
        for g, w in enumerate(POOL_WINDOWS[1:], start=1):
            for i in range(w // 2, w):
                acc = acc + shifted(i)
            total = jnp.where(lane >= g * POOL_GROUP, acc, total)
        count = jnp.minimum(pos0 + r0 + row + 1, window).astype(_F32)
        diff = total / count - cur
        y = jnp.dot(diff.astype(_BF16), w_ref[...], preferred_element_type=_F32)
        y = (y + b_ref[...]) * sc_ref[...]
        if bb == 1:
            out_ref[pl.ds(pl.multiple_of(r0, chunk), chunk), :] = y.astype(out_ref.dtype)
        else:
            out_ref[...] = y.astype(out_ref.dtype)
        return carry
    lax.fori_loop(0, seq // chunk, body, 0)


def _pool(ext, w_bd, bias, scale, pos0, bb, chunk):
    batch, rows, _ = ext.shape
    seq = rows - POOL_HALO
    assert bb == 1 or seq == chunk
    return pl.pallas_call(
        functools.partial(_pool_kernel, bb=bb, seq=seq, chunk=chunk, pos0=pos0),
        grid=(batch // bb,),
        in_specs=[pl.BlockSpec((bb, rows, POOL_WIDTH), lambda i: (i, 0, 0)),
                  pl.BlockSpec((POOL_WIDTH, POOL_WIDTH), lambda i: (0, 0)),
                  pl.BlockSpec((1, POOL_WIDTH), lambda i: (0, 0)),
                  pl.BlockSpec((1, POOL_WIDTH), lambda i: (0, 0))],
        out_specs=pl.BlockSpec((bb * seq, POOL_WIDTH), lambda i: (i, 0)),
        out_shape=jax.ShapeDtypeStruct((batch * seq, POOL_WIDTH), _BF16),
        scratch_shapes=[pltpu.VMEM((bb, chunk + POOL_HALO, POOL_WIDTH), _F32)],
        compiler_params=_params(1, 48),
        name="pool",
    )(ext, w_bd, bias, scale)


def _mix_kernel(a_ref, p_ref, x_ref, w_ref, g_ref, b_ref, out_ref, *, alpha):
    mix = (jnp.dot(a_ref[...], w_ref[0:ATTN_WIDTH, :], preferred_element_type=_F32)
           + jnp.dot(p_ref[...], w_ref[ATTN_WIDTH:, :], preferred_element_type=_F32))
    out_ref[...] = _layer_norm(alpha * x_ref[...] + mix, g_ref[...], b_ref[...])


def _mix(a, p, x, w_out_bf, g, b, alpha, tm):
    n, d = x.shape
    return pl.pallas_call(
        functools.partial(_mix_kernel, alpha=alpha),
        grid=(n // tm,),
        in_specs=[pl.BlockSpec((tm, ATTN_WIDTH), lambda i: (i, 0)),
                  pl.BlockSpec((tm, POOL_WIDTH), lambda i: (i, 0)),
                  pl.BlockSpec((tm, d), lambda i: (i, 0)),
                  pl.BlockSpec((ATTN_WIDTH + POOL_WIDTH, d), lambda i: (0, 0)),
                  pl.BlockSpec((1, d), lambda i: (0, 0)),
                  pl.BlockSpec((1, d), lambda i: (0, 0))],
        out_specs=pl.BlockSpec((tm, d), lambda i: (i, 0)),
        out_shape=jax.ShapeDtypeStruct((n, d), _F32),
        compiler_params=_params(1, 48),
        name="mix",
    )(a, p, x, w_out_bf, g, b)


_ROUTER_LANES = LANES
_EXPERT_LANE0 = N_EXPERT_GROUPS


def _router_combine(h, wr_ref, br_ref):
    logits = jnp.dot(h, wr_ref[...], preferred_element_type=_F32,
                     precision=lax.Precision.HIGHEST) + br_ref[...]
    lane = lax.broadcasted_iota(jnp.int32, logits.shape, 1)
    big = _ROUTER_LANES
    g_logit = jnp.where(lane < N_EXPERT_GROUPS, logits, NEG_INF)
    g_max = jnp.max(g_logit, axis=-1, keepdims=True)
    g_sel = jnp.min(jnp.where(g_logit == g_max, lane, big), axis=-1, keepdims=True)
    p_group = 1.0 / jnp.sum(jnp.exp(g_logit - g_max), axis=-1, keepdims=True)
    lane_group = (lane - _EXPERT_LANE0) >> 2
    in_group = ((lane >= _EXPERT_LANE0) & (lane < _EXPERT_LANE0 + N_EXPERTS)
                & (lane_group == g_sel))
    e_logit = jnp.where(in_group, logits, NEG_INF)
    v1 = jnp.max(e_logit, axis=-1, keepdims=True)
    i1 = jnp.min(jnp.where(e_logit == v1, lane, big), axis=-1, keepdims=True)
    e_rest = jnp.where(lane == i1, NEG_INF, e_logit)
    v2 = jnp.max(e_rest, axis=-1, keepdims=True)
    i2 = jnp.min(jnp.where(e_rest == v2, lane, big), axis=-1, keepdims=True)
    t = jnp.exp(v2 - v1)
    w1 = 1.0 / (1.0 + t)
    w2 = t / (1.0 + t)
    return p_group * (jnp.where(lane == i1, w1, 0.0) + jnp.where(lane == i2, w2, 0.0))


def _moe_kernel(h_ref, wr_ref, br_ref, wgu_ref, wd_ref, g_ref, b_ref, out_ref,
                hb_scr, comb_scr, acc_scr, *, alpha):
    e = pl.program_id(1)

    @pl.when(e == 0)
    def _():
        h = h_ref[...]
        hb_scr[...] = h.astype(_BF16)
        comb_scr[...] = _router_combine(h, wr_ref, br_ref)
        acc_scr[...] = jnp.zeros_like(acc_scr)

    comb = comb_scr[...]
    lane = lax.broadcasted_iota(jnp.int32, comb.shape, 1)
    gate = jnp.sum(jnp.where(lane == _EXPERT_LANE0 + e, comb, 0.0), axis=-1, keepdims=True)
    hgu = jnp.dot(hb_scr[...], wgu_ref[...], preferred_element_type=_F32)
    hg = hgu[:, :D_EXPERT]
    hu = hgu[:, D_EXPERT:]
    act = hg * (1.0 / (1.0 + jnp.exp(-hg))) * hu * gate
    acc_scr[...] += jnp.dot(act.astype(_BF16), wd_ref[...], preferred_element_type=_F32)

    @pl.when(e == N_EXPERTS - 1)
    def _():
        out_ref[...] = _layer_norm(alpha * h_ref[...] + acc_scr[...], g_ref[...], b_ref[...])


def _moe(h, w_router, b_router, w_gu_bf, w_down_bf, g, b, alpha, tm):
    n, d = h.shape
    return pl.pallas_call(
        functools.partial(_moe_kernel, alpha=alpha),
        grid=(n // tm, N_EXPERTS),
        in_specs=[pl.BlockSpec((tm, d), lambda i, e: (i, 0)),
                  pl.BlockSpec((d, _ROUTER_LANES), lambda i, e: (0, 0)),
                  pl.BlockSpec((1, _ROUTER_LANES), lambda i, e: (0, 0)),
                  pl.BlockSpec((None, d, 2 * D_EXPERT), lambda i, e: (e, 0, 0)),
                  pl.BlockSpec((None, D_EXPERT, d), lambda i, e: (e, 0, 0)),
                  pl.BlockSpec((1, d), lambda i, e: (0, 0)),
                  pl.BlockSpec((1, d), lambda i, e: (0, 0))],
        out_specs=pl.BlockSpec((tm, d), lambda i, e: (i, 0)),
        out_shape=jax.ShapeDtypeStruct((n, d), _F32),
        scratch_shapes=[pltpu.VMEM((tm, d), _BF16),
                        pltpu.VMEM((tm, _ROUTER_LANES), _F32),
                        pltpu.VMEM((tm, d), _F32)],
        compiler_params=_params(2, 48),
        name="moe",
    )(h, w_router, b_router, w_gu_bf, w_down_bf, g, b)


def _block_diag(w_pool):
    n = len(POOL_WINDOWS)
    eye = jnp.eye(n, dtype=w_pool.dtype)
    return jnp.einsum('gce,gh->gche', w_pool, eye).reshape(POOL_WIDTH, POOL_WIDTH)


def _router_weights(w_rg, b_rg, w_re, b_re):
    d = w_rg.shape[0]
    w_e = jnp.transpose(w_re, (1, 0, 2)).reshape(d, N_EXPERTS)
    w = jnp.concatenate([w_rg, w_e], axis=1)
    w = jnp.pad(w, ((0, 0), (0, _ROUTER_LANES - w.shape[1])))
    bias = jnp.concatenate([b_rg, b_re.reshape(N_EXPERTS)])
    bias = jnp.pad(bias, (0, _ROUTER_LANES - bias.shape[0])).reshape(1, _ROUTER_LANES)
    return w, bias


def _class_major(x, batch, seq, r):
    w = x.shape[-1]
    return x.reshape(batch, seq // r, r, w).transpose(0, 2, 1, 3).reshape(batch * r, seq // r, w)


def _row_tile(n, target):
    return target if n % target == 0 else n


def kernel(x_prompt, x_sample, cache_k, cache_v, state_pool, w_in, w_out, w_pool, b_pool,
           pool_scale, ln1_g, ln1_b, w_rg, b_rg, w_re, b_re, w_gate, w_up, w_down, ln2_g, ln2_b):
    depth = w_in.shape[0]
    alpha = float((2 * depth) ** 0.25)
    bp, tp, d = x_prompt.shape
    bs, ts, _ = x_sample.shape
    kv_rows = cache_k.shape[2]
    assert kv_rows == MAX_WINDOW and tp % SUPER_BLOCK == 0 and ts == SUBLANES
    win_p = min(MAX_WINDOW, tp)

    xp = x_prompt.reshape(bp * tp, d)
    xs = x_sample.reshape(bs * ts, d)
    ckt = jnp.transpose(cache_k, (0, 1, 3, 4, 2))
    cvt = jnp.transpose(cache_v, (0, 1, 3, 4, 2))

    pk, pv, pu, sk, sv, su = [], [], [], [], [], []
    for l in range(depth):
        w_in_bf = w_in[l].astype(_BF16)
        w_out_bf = w_out[l].astype(_BF16)
        w_bd = _block_diag(w_pool[l]).astype(_BF16)
        b_pl = b_pool[l].reshape(1, POOL_WIDTH)
        sc_pl = pool_scale[l].reshape(1, POOL_WIDTH)
        w_router, b_router = _router_weights(w_rg[l], b_rg[l], w_re[l], b_re[l])
        w_gu_bf = jnp.concatenate([w_gate[l], w_up[l]], axis=-1).astype(_BF16)
        w_down_bf = w_down[l].astype(_BF16)
        g1, b1 = ln1_g[l].reshape(1, d), ln1_b[l].reshape(1, d)
        g2, b2 = ln2_g[l].reshape(1, d), ln2_b[l].reshape(1, d)

        proj_p, qkv_p = _proj(xp, w_in_bf, _row_tile(bp * tp, 512))
        a_p = _attn_prompt(qkv_p, _class_major(qkv_p, bp, tp, 4),
                           _class_major(qkv_p, bp, tp, 16), bp, tp)
        proj_p3 = proj_p.reshape(bp, tp, PROJ_WIDTH)
        u_p = proj_p3[:, :, QKV_WIDTH:]
        ext_p = jnp.concatenate([jnp.zeros((bp, POOL_HALO, POOL_WIDTH), _F32), u_p], axis=1)
        p_p = _pool(ext_p, w_bd, b_pl, sc_pl, 0, 1, 128)
        h_p = _mix(a_p, p_p, xp, w_out_bf, g1, b1, alpha, _row_tile(bp * tp, 512))
        xp = _moe(h_p, w_router, b_router, w_gu_bf, w_down_bf, g2, b2, alpha,
                  _row_tile(bp * tp, 1024))
        pk.append(proj_p3[:, tp - win_p:, ATTN_WIDTH:2 * ATTN_WIDTH]
                  .reshape(bp, win_p, N_HEADS, HEAD_DIM))
        pv.append(proj_p3[:, tp - win_p:, 2 * ATTN_WIDTH:QKV_WIDTH]
                  .reshape(bp, win_p, N_HEADS, HEAD_DIM))
        pu.append(u_p[:, tp - POOL_PREFIX:])

        proj_s, _ = _proj(xs, w_in_bf, bs * ts)
        a_s = _attn_sample(proj_s, ckt, cvt, l, bs, ts)
        proj_s3 = proj_s.reshape(bs, ts, PROJ_WIDTH)
        u_s = proj_s3[:, :, QKV_WIDTH:]
        ext_s = jnp.concatenate(
            [jnp.zeros((bs, POOL_HALO - POOL_PREFIX, POOL_WIDTH), _F32), state_pool[l], u_s],
            axis=1)
        p_s = _pool(ext_s, w_bd, b_pl, sc_pl, PAST_LEN, bs, ts)
        h_s = _mix(a_s.astype(_BF16), p_s, xs, w_out_bf, g1, b1, alpha, bs * ts)
        xs = _moe(h_s, w_router, b_router, w_gu_bf, w_down_bf, g2, b2, alpha, bs * ts)
        sk.append(proj_s3[:, :, ATTN_WIDTH:2 * ATTN_WIDTH].reshape(bs, ts, N_HEADS, HEAD_DIM))
        sv.append(proj_s3[:, :, 2 * ATTN_WIDTH:QKV_WIDTH].reshape(bs, ts, N_HEADS, HEAD_DIM))
        su.append(u_s)

    return (xp.reshape(bp, tp, d), xs.reshape(bs, ts, d), jnp.stack(pk), jnp.stack(pv),
            jnp.stack(pu), jnp.stack(sk), jnp.stack(sv), jnp.stack(su))
```

```python
import functools
import math

import jax
import jax.numpy as jnp
from jax import lax
from jax.experimental import pallas as pl
from jax.experimental.pallas import tpu as pltpu

N_HEADS = 12
HEAD_DIM = 64
ATTN_WIDTH = N_HEADS * HEAD_DIM
POOL_WIDTH = 256
PROJ_WIDTH = 3 * ATTN_WIDTH + POOL_WIDTH
QKV_WIDTH = 3 * ATTN_WIDTH
POOL_WINDOWS = (2, 4, 8, 16)
POOL_GROUP = POOL_WIDTH // len(POOL_WINDOWS)
POOL_PREFIX = max(POOL_WINDOWS) - 1
POOL_HALO = 16
DILATIONS = (1, 4, 16)
WINDOW_STEPS = 128
MAX_WINDOW = 2048
PAST_LEN = 8192
N_EXPERT_GROUPS = 4
EXPERTS_PER_GROUP = 4
N_EXPERTS = 16
D_EXPERT = 256
LN_EPS = 1e-5
NEG_INF = -1e30
QK_SCALE = HEAD_DIM ** -0.5
LOG2_E = math.log2(math.e)

LANES = 128
SUBLANES = 8
HEADS_PER_TILE = LANES // HEAD_DIM
N_HEAD_TILES = ATTN_WIDTH // LANES
MXU_WIDTH = 256

ATTN_BLOCK = WINDOW_STEPS
SUPER_BLOCK = ATTN_BLOCK * max(DILATIONS)
BLOCKS_PER_SUPER = SUPER_BLOCK // ATTN_BLOCK
ATTN_UNROLL = 16

_F32 = jnp.float32
_BF16 = jnp.bfloat16
_NT = (((1,), (1,)), ((), ()))


def _params(n_grid_dims, vmem_mib):
    return pltpu.CompilerParams(
        dimension_semantics=("arbitrary",) * n_grid_dims,
        vmem_limit_bytes=vmem_mib * 1024 * 1024)


def _layer_norm(z, g, b):
    mu = jnp.mean(z, axis=-1, keepdims=True)
    zc = z - mu
    var = jnp.mean(zc * zc, axis=-1, keepdims=True)
    return zc * lax.rsqrt(var + LN_EPS) * g + b


def _proj_sample_kernel(x_ref, w_ref, of_ref):
    xb = x_ref[...].astype(_BF16)
    for j in range(0, PROJ_WIDTH, MXU_WIDTH):
        of_ref[:, j:j + MXU_WIDTH] = jnp.dot(xb, w_ref[:, j:j + MXU_WIDTH],
                                             preferred_element_type=_F32)


def _proj_sample(x, w_in_bf):
    n, d = x.shape
    return pl.pallas_call(
        _proj_sample_kernel,
        grid=(1,),
        in_specs=[pl.BlockSpec((n, d), lambda i: (0, 0)),
                  pl.BlockSpec((d, PROJ_WIDTH), lambda i: (0, 0))],
        out_specs=pl.BlockSpec((n, PROJ_WIDTH), lambda i: (0, 0)),
        out_shape=jax.ShapeDtypeStruct((n, PROJ_WIDTH), _F32),
        compiler_params=_params(1, 48),
        name="proj_sample",
    )(x, w_in_bf)


def _proj_prompt_kernel(x_ref, w_ref, of_ref, ob_ref, o4_ref, o16_ref, slab_scr, cls_scr):
    tm = x_ref.shape[0]
    xb = x_ref[...].astype(_BF16)
    n_slab = 0
    for j in range(0, PROJ_WIDTH, MXU_WIDTH):
        r = jnp.dot(xb, w_ref[:, j:j + MXU_WIDTH], preferred_element_type=_F32)
        if j >= ATTN_WIDTH:
            of_ref[:, j - ATTN_WIDTH:j - ATTN_WIDTH + MXU_WIDTH] = r
        if j >= QKV_WIDTH:
            continue
        if j < ATTN_WIDTH:
            r = r * (QK_SCALE * LOG2_E)
        ob_ref[:, j:j + MXU_WIDTH] = r.astype(_BF16)
        for half in range(MXU_WIDTH // LANES):
            lanes = slice(j + half * LANES, j + (half + 1) * LANES)
            buf = n_slab % 2
            n_slab += 1
            slab_scr[buf] = r[:, half * LANES:(half + 1) * LANES]
            for c4 in range(4):
                x4 = slab_scr[buf, pl.ds(c4, tm // 4, stride=4), :]
                o4_ref[c4, :, lanes] = x4.astype(_BF16)
                cls_scr[buf, c4] = x4
                for c in range(4):
                    x16 = cls_scr[buf, c4, pl.ds(c, tm // 16, stride=4), :]
                    o16_ref[4 * c + c4, :, lanes] = x16.astype(_BF16)


def _proj_prompt(x, w_in_bf, batch, seq, tm):
    n, d = x.shape
    per_batch = seq // tm
    kvu = PROJ_WIDTH - ATTN_WIDTH

    def cls(r):
        return pl.BlockSpec((r, tm // r, QKV_WIDTH),
                            lambda i: (i // per_batch, i % per_batch, 0))
    return pl.pallas_call(
        _proj_prompt_kernel,
        grid=(n // tm,),
        in_specs=[pl.BlockSpec((tm, d), lambda i: (i, 0)),
                  pl.BlockSpec((d, PROJ_WIDTH), lambda i: (0, 0))],
        out_specs=[pl.BlockSpec((tm, kvu), lambda i: (i, 0)),
                   pl.BlockSpec((tm, QKV_WIDTH), lambda i: (i, 0)),
                   cls(4), cls(16)],
        out_shape=[jax.ShapeDtypeStruct((n, kvu), _F32),
                   jax.ShapeDtypeStruct((n, QKV_WIDTH), _BF16),
                   jax.ShapeDtypeStruct((batch * 4, seq // 4, QKV_WIDTH), _BF16),
                   jax.ShapeDtypeStruct((batch * 16, seq // 16, QKV_WIDTH), _BF16)],
        scratch_shapes=[pltpu.VMEM((2, tm, LANES), _F32),
                        pltpu.VMEM((2, 4, tm // 4, LANES), _F32)],
        compiler_params=_params(1, 48),
        name="proj_prompt",
    )(x, w_in_bf)


def _attn_prompt_kernel(q1, q4, q16, k1, k1h, k4, k4h, k16, k16h,
                        v1, v1h, v4, v4h, v16, v16h, out_ref,
                        kc1, kc4, kc16, vc1, vc4, vc16,
                        o_scr, m_scr, l_scr, bias_scr):
    blk = ATTN_BLOCK
    first_super = pl.program_id(2) == 0
    ii = lax.broadcasted_iota(jnp.int32, (blk, 2 * blk), 0)
    jj = lax.broadcasted_iota(jnp.int32, (blk, 2 * blk), 1)
    band = (jj >= ii) & (jj <= ii + WINDOW_STEPS)
    lo = jnp.where(first_super, blk, 0)
    bias_scr[0] = jnp.where(band, 0.0, NEG_INF)
    bias_scr[1] = jnp.where(band & (jj >= lo), 0.0, NEG_INF)
    head0 = lax.broadcasted_iota(jnp.int32, (blk, LANES), 1) < HEAD_DIM

    for halo, main, cat in ((k1h, k1, kc1), (k4h, k4, kc4), (k16h, k16, kc16),
                            (v1h, v1, vc1), (v4h, v4, vc4), (v16h, v16, vc16)):
        cat[:, 0:blk, :] = halo[...]
        cat[:, blk:, :] = main[...]

    def block(g, r, q_ref, kc, vc, c, ib):
        if isinstance(ib, int):
            off = ib * blk
            bias = bias_scr[1 if ib == 0 else 0]
        else:
            off = pl.multiple_of(ib * blk, blk)
            bias = bias_scr[jnp.where(ib == 0, 1, 0)]
        q = q_ref[c, pl.ds(off, blk), :]
        k2 = kc[c, pl.ds(off, 2 * blk), :]
        v2 = vc[c, pl.ds(off, 2 * blk), :]
        zero = jnp.zeros_like(q)
        qq = jnp.concatenate([jnp.where(head0, q, zero), jnp.where(head0, zero, q)], axis=0)
        s = lax.dot_general(qq, k2, _NT, preferred_element_type=_F32)
        ms, ls, ps = [], [], []
        for h in range(HEADS_PER_TILE):
            sh = s[h * blk:(h + 1) * blk, :] + bias
            m = jnp.max(sh, axis=-1, keepdims=True)
            p = jnp.exp2(sh - m)
            ms.append(m)
            ls.append(jnp.sum(p, axis=-1, keepdims=True))
            ps.append(p.astype(_BF16))
        o = jnp.dot(jnp.concatenate(ps, axis=0), v2, preferred_element_type=_F32)
        if r == 1:
            rows = pl.ds(off, blk)
        else:
            rows = pl.ds(c + r * off, blk, stride=r)
        m_scr[g, rows, :] = jnp.where(head0, ms[0], ms[1])
        l_scr[g, rows, :] = jnp.where(head0, ls[0], ls[1])
        o_scr[g, rows, :] = jnp.where(head0, o[0:blk, :], o[blk:2 * blk, :])

    def branch(g, r, q_ref, kc, vc):
        per_class = BLOCKS_PER_SUPER // r
        assert r == 1 or ATTN_UNROLL % per_class == 0

        def body(j, carry):
            for u in range(ATTN_UNROLL):
                if r == 1:
                    c, ib = 0, j * ATTN_UNROLL + u
                else:
                    c = j * (ATTN_UNROLL // per_class) + u // per_class
                    ib = u % per_class
                block(g, r, q_ref, kc, vc, c, ib)
            return carry
        lax.fori_loop(0, BLOCKS_PER_SUPER // ATTN_UNROLL, body, 0)

    branch(0, 1, q1, kc1, vc1)
    branch(1, 4, q4, kc4, vc4)
    branch(2, 16, q16, kc16, vc16)

    chunk = 256

    def merge(ci, carry):
        rows = pl.ds(pl.multiple_of(ci * chunk, chunk), chunk)
        ms = [m_scr[g, rows, :] for g in range(3)]
        m = jnp.maximum(jnp.maximum(ms[0], ms[1]), ms[2])
        num = jnp.zeros((chunk, LANES), _F32)
        den = jnp.zeros((chunk, LANES), _F32)
        for g in range(3):
            a = jnp.exp2(ms[g] - m)
            num = num + a * o_scr[g, rows, :]
            den = den + a * l_scr[g, rows, :]
        out_ref[rows, :] = (num / den).astype(out_ref.dtype)
        return carry
    lax.fori_loop(0, SUPER_BLOCK // chunk, merge, 0)


def _attn_prompt(qkv, qkv4, qkv16, batch, seq):
    blk, sb = ATTN_BLOCK, SUPER_BLOCK
    n_super = seq // sb
    kcol, vcol = N_HEAD_TILES, 2 * N_HEAD_TILES
    views = {1: qkv.reshape(batch, seq, QKV_WIDTH), 4: qkv4, 16: qkv16}

    def main(r, col):
        return pl.BlockSpec((r, sb // r, LANES), lambda b, hp, s: (b, s, col + hp))

    def halo(r, col):
        per = sb // (r * blk)
        return pl.BlockSpec((r, blk, LANES),
                            lambda b, hp, s: (b, jnp.maximum(s * per - 1, 0), col + hp))

    in_specs = [main(r, 0) for r in DILATIONS]
    operands = [views[r] for r in DILATIONS]
    for col in (kcol, vcol):
        for r in DILATIONS:
            in_specs += [main(r, col), halo(r, col)]
            operands += [views[r], views[r]]
    cat_shapes = [pltpu.VMEM((r, sb // r + blk, LANES), _BF16) for r in DILATIONS]
    return pl.pallas_call(
        _attn_prompt_kernel,
        grid=(batch, N_HEAD_TILES, n_super),
        in_specs=in_specs,
        out_specs=pl.BlockSpec((sb, LANES), lambda b, hp, s: (b * n_super + s, hp)),
        out_shape=jax.ShapeDtypeStruct((batch * seq, ATTN_WIDTH), _BF16),
        scratch_shapes=cat_shapes + cat_shapes + [
            pltpu.VMEM((3, sb, LANES), _F32),
            pltpu.VMEM((3, sb, LANES), _F32),
            pltpu.VMEM((3, sb, LANES), _F32),
            pltpu.VMEM((2, blk, 2 * blk), _F32)],
        compiler_params=_params(3, 48),
        name="attn_prompt",
    )(*operands)


def _branch_count(dist):
    return ((dist <= 128).astype(_F32)
            + (((dist & 3) == 0) & (dist <= 512)).astype(_F32)
            + (((dist & 15) == 0) & (dist <= MAX_WINDOW)).astype(_F32))


def _attn_sample_kernel(proj_ref, kt_ref, vt_ref, out_ref, *, n_new):
    rows = HEADS_PER_TILE * n_new
    t_c = lax.broadcasted_iota(jnp.int32, (rows, MAX_WINDOW), 0) & (n_new - 1)
    f_c = lax.broadcasted_iota(jnp.int32, (rows, MAX_WINDOW), 1)
    w_c = _branch_count(MAX_WINDOW + t_c - f_c)
    t_n = lax.broadcasted_iota(jnp.int32, (rows, LANES), 0) & (n_new - 1)
    f_n = lax.broadcasted_iota(jnp.int32, (rows, LANES), 1)
    d_n = t_n - f_n
    w_n = jnp.where((d_n >= 0) & (f_n < n_new), _branch_count(d_n), 0.0)
    head0 = lax.broadcasted_iota(jnp.int32, (n_new, LANES), 1) < HEAD_DIM
    pad = jnp.zeros((LANES - n_new, LANES), _F32)

    for hp in range(N_HEAD_TILES):
        sl = slice(hp * LANES, (hp + 1) * LANES)
        ksl = slice(ATTN_WIDTH + hp * LANES, ATTN_WIDTH + (hp + 1) * LANES)
        vsl = slice(2 * ATTN_WIDTH + hp * LANES, 2 * ATTN_WIDTH + (hp + 1) * LANES)
        heads = slice(hp * HEADS_PER_TILE, (hp + 1) * HEADS_PER_TILE)
        q = proj_ref[:, sl] * QK_SCALE
        q2 = jnp.concatenate([jnp.where(head0, q, 0.0),
                              jnp.where(head0, 0.0, q)], axis=0).astype(_BF16)
        kt = kt_ref[heads].reshape(LANES, MAX_WINDOW).astype(_BF16)
        vt = vt_ref[heads].reshape(LANES, MAX_WINDOW).astype(_BF16)
        k_n = jnp.concatenate([proj_ref[:, ksl], pad], axis=0).astype(_BF16)
        v_n = jnp.concatenate([proj_ref[:, vsl], pad], axis=0).astype(_BF16)
        s_c = jnp.dot(q2, kt, preferred_element_type=_F32)
        s_n = lax.dot_general(q2, k_n, _NT, preferred_element_type=_F32)
        s_c = jnp.where(w_c > 0.0, s_c, NEG_INF)
        s_n = jnp.where(w_n > 0.0, s_n, NEG_INF)
        m = jnp.maximum(jnp.max(s_c, axis=-1, keepdims=True),
                        jnp.max(s_n, axis=-1, keepdims=True))
        p_c = w_c * jnp.exp(s_c - m)
        p_n = w_n * jnp.exp(s_n - m)
        l = jnp.sum(p_c, axis=-1, keepdims=True) + jnp.sum(p_n, axis=-1, keepdims=True)
        o = (lax.dot_general(p_c.astype(_BF16), vt, _NT, preferred_element_type=_F32)
             + jnp.dot(p_n.astype(_BF16), v_n, preferred_element_type=_F32)) / l
        out_ref[:, sl] = jnp.where(head0, o[0:n_new, :], o[n_new:rows, :])


def _attn_sample(proj_f32, cache_kt, cache_vt, layer, batch, n_new):
    spec = pl.BlockSpec((None, None, N_HEADS, HEAD_DIM, MAX_WINDOW),
                        lambda b: (layer, b, 0, 0, 0))
    return pl.pallas_call(
        functools.partial(_attn_sample_kernel, n_new=n_new),
        grid=(batch,),
        in_specs=[pl.BlockSpec((n_new, PROJ_WIDTH), lambda b: (b, 0)), spec, spec],
        out_specs=pl.BlockSpec((n_new, ATTN_WIDTH), lambda b: (b, 0)),
        out_shape=jax.ShapeDtypeStruct((batch * n_new, ATTN_WIDTH), _F32),
        compiler_params=_params(1, 48),
        name="attn_sample",
    )(proj_f32, cache_kt, cache_vt)


def _pool_kernel(ext_ref, w_ref, b_ref, sc_ref, out_ref, win_scr, *, bb, seq, chunk, pos0):
    n = bb * chunk
    lane = lax.broadcasted_iota(jnp.int32, (n, POOL_WIDTH), 1)
    window = jnp.where(lane < POOL_GROUP, 2,
                       jnp.where(lane < 2 * POOL_GROUP, 4,
                                 jnp.where(lane < 3 * POOL_GROUP, 8, 16)))
    row = lax.broadcasted_iota(jnp.int32, (bb, chunk, POOL_WIDTH), 1).reshape(n, POOL_WIDTH)

    def body(ci, carry):
        r0 = pl.multiple_of(ci * chunk, SUBLANES)
        win_scr[...] = ext_ref[:, pl.ds(r0, chunk + POOL_HALO), :]

        def shifted(i):
            return win_scr[:, POOL_HALO - i:POOL_HALO - i + chunk, :].reshape(n, POOL_WIDTH)
        cur = shifted(0)
        acc = cur + shifted(1)
        total = acc
        for g, w in enumerate(POOL_WINDOWS[1:], start=1):
            for i in range(w // 2, w):
                acc = acc + shifted(i)
            total = jnp.where(lane >= g * POOL_GROUP, acc, total)
        count = jnp.minimum(pos0 + r0 + row + 1, window).astype(_F32)
        diff = total / count - cur
        y = jnp.dot(diff.astype(_BF16), w_ref[...], preferred_element_type=_F32)
        y = (y + b_ref[...]) * sc_ref[...]
        if bb == 1:
            out_ref[pl.ds(pl.multiple_of(r0, chunk), chunk), :] = y.astype(out_ref.dtype)
        else:
            out_ref[...] = y.astype(out_ref.dtype)
        return carry
    lax.fori_loop(0, seq // chunk, body, 0)


def _pool(ext, w_bd, bias, scale, pos0, bb, chunk):
    batch, rows, _ = ext.shape
    seq = rows - POOL_HALO
    assert bb == 1 or seq == chunk
    return pl.pallas_call(
        functools.partial(_pool_kernel, bb=bb, seq=seq, chunk=chunk, pos0=pos0),
        grid=(batch // bb,),
        in_specs=[pl.BlockSpec((bb, rows, POOL_WIDTH), lambda i: (i, 0, 0)),
                  pl.BlockSpec((POOL_WIDTH, POOL_WIDTH), lambda i: (0, 0)),
                  pl.BlockSpec((1, POOL_WIDTH), lambda i: (0, 0)),
                  pl.BlockSpec((1, POOL_WIDTH), lambda i: (0, 0))],
        out_specs=pl.BlockSpec((bb * seq, POOL_WIDTH), lambda i: (i, 0)),
        out_shape=jax.ShapeDtypeStruct((batch * seq, POOL_WIDTH), _BF16),
        scratch_shapes=[pltpu.VMEM((bb, chunk + POOL_HALO, POOL_WIDTH), _F32)],
        compiler_params=_params(1, 48),
        name="pool",
    )(ext, w_bd, bias, scale)


def _mix_kernel(a_ref, p_ref, x_ref, w_ref, g_ref, b_ref, out_ref, *, alpha):
    mix = (jnp.dot(a_ref[...], w_ref[0:ATTN_WIDTH, :], preferred_element_type=_F32)
           + jnp.dot(p_ref[...], w_ref[ATTN_WIDTH:, :], preferred_element_type=_F32))
    out_ref[...] = _layer_norm(alpha * x_ref[...] + mix, g_ref[...], b_ref[...])


def _mix(a, p, x, w_out_bf, g, b, alpha, tm):
    n, d = x.shape
    return pl.pallas_call(
        functools.partial(_mix_kernel, alpha=alpha),
        grid=(n // tm,),
        in_specs=[pl.BlockSpec((tm, ATTN_WIDTH), lambda i: (i, 0)),
                  pl.BlockSpec((tm, POOL_WIDTH), lambda i: (i, 0)),
                  pl.BlockSpec((tm, d), lambda i: (i, 0)),
                  pl.BlockSpec((ATTN_WIDTH + POOL_WIDTH, d), lambda i: (0, 0)),
                  pl.BlockSpec((1, d), lambda i: (0, 0)),
                  pl.BlockSpec((1, d), lambda i: (0, 0))],
        out_specs=pl.BlockSpec((tm, d), lambda i: (i, 0)),
        out_shape=jax.ShapeDtypeStruct((n, d), _F32),
        compiler_params=_params(1, 48),
        name="mix",
    )(a, p, x, w_out_bf, g, b)


_ROUTER_LANES = LANES
_EXPERT_LANE0 = N_EXPERT_GROUPS


def _router_combine(h, hb, wr_ref, br_ref):
    h_lo = (h - hb.astype(_F32)).astype(_BF16)
    logits = (jnp.dot(hb, wr_ref[0], preferred_element_type=_F32)
              + jnp.dot(h_lo, wr_ref[0], preferred_element_type=_F32)
              + jnp.dot(hb, wr_ref[1], preferred_element_type=_F32)) + br_ref[...]
    lane = lax.broadcasted_iota(jnp.int32, logits.shape, 1)
    big = _ROUTER_LANES
    g_logit = jnp.where(lane < N_EXPERT_GROUPS, logits, NEG_INF)
    g_max = jnp.max(g_logit, axis=-1, keepdims=True)
    g_sel = jnp.min(jnp.where(g_logit == g_max, lane, big), axis=-1, keepdims=True)
    p_group = 1.0 / jnp.sum(jnp.exp(g_logit - g_max), axis=-1, keepdims=True)
    lane_group = (lane - _EXPERT_LANE0) >> 2
    in_group = ((lane >= _EXPERT_LANE0) & (lane < _EXPERT_LANE0 + N_EXPERTS)
                & (lane_group == g_sel))
    e_logit = jnp.where(in_group, logits, NEG_INF)
    v1 = jnp.max(e_logit, axis=-1, keepdims=True)
    i1 = jnp.min(jnp.where(e_logit == v1, lane, big), axis=-1, keepdims=True)
    e_rest = jnp.where(lane == i1, NEG_INF, e_logit)
    v2 = jnp.max(e_rest, axis=-1, keepdims=True)
    i2 = jnp.min(jnp.where(e_rest == v2, lane, big), axis=-1, keepdims=True)
    t = jnp.exp(v2 - v1)
    w1 = 1.0 / (1.0 + t)
    w2 = t / (1.0 + t)
    return p_group * (jnp.where(lane == i1, w1, 0.0) + jnp.where(lane == i2, w2, 0.0))


def _moe_kernel(h_ref, wr_ref, br_ref, wg_ref, wu_ref, wd_ref, g_ref, b_ref, out_ref,
                hb_scr, comb_scr, *, alpha):
    e = pl.program_id(1)

    @pl.when(e == 0)
    def _():
        h = h_ref[...]
        hb = h.astype(_BF16)
        hb_scr[...] = hb
        comb_scr[...] = _router_combine(h, hb, wr_ref, br_ref)
        out_ref[...] = jnp.zeros_like(out_ref)

    comb = comb_scr[...]
    lane = lax.broadcasted_iota(jnp.int32, comb.shape, 1)
    gate = jnp.sum(jnp.where(lane == _EXPERT_LANE0 + e, comb, 0.0), axis=-1, keepdims=True)
    hb = hb_scr[...]
    hg = jnp.dot(hb, wg_ref[...].astype(_BF16), preferred_element_type=_F32)
    hu = jnp.dot(hb, wu_ref[...].astype(_BF16), preferred_element_type=_F32)
    act = hg * (1.0 / (1.0 + jnp.exp(-hg))) * hu * gate
    out_ref[...] += jnp.dot(act.astype(_BF16), wd_ref[...].astype(_BF16),
                            preferred_element_type=_F32)

    @pl.when(e == N_EXPERTS - 1)
    def _():
        out_ref[...] = _layer_norm(alpha * h_ref[...] + out_ref[...], g_ref[...], b_ref[...])


def _moe(h, w_router, b_router, w_gate, w_up, w_down, layer, g, b, alpha, tm):
    n, d = h.shape
    return pl.pallas_call(
        functools.partial(_moe_kernel, alpha=alpha),
        grid=(n // tm, N_EXPERTS),
        in_specs=[pl.BlockSpec((tm, d), lambda i, e: (i, 0)),
                  pl.BlockSpec((2, d, _ROUTER_LANES), lambda i, e: (0, 0, 0)),
                  pl.BlockSpec((1, _ROUTER_LANES), lambda i, e: (0, 0)),
                  pl.BlockSpec((None, None, d, D_EXPERT), lambda i, e: (layer, e, 0, 0)),
                  pl.BlockSpec((None, None, d, D_EXPERT), lambda i, e: (layer, e, 0, 0)),
                  pl.BlockSpec((None, None, D_EXPERT, d), lambda i, e: (layer, e, 0, 0)),
                  pl.BlockSpec((1, d), lambda i, e: (0, 0)),
                  pl.BlockSpec((1, d), lambda i, e: (0, 0))],
        out_specs=pl.BlockSpec((tm, d), lambda i, e: (i, 0)),
        out_shape=jax.ShapeDtypeStruct((n, d), _F32),
        scratch_shapes=[pltpu.VMEM((tm, d), _BF16),
                        pltpu.VMEM((tm, _ROUTER_LANES), _F32)],
        compiler_params=_params(2, 60),
        name="moe",
    )(h, w_router, b_router, w_gate, w_up, w_down, g, b)


def _block_diag(w_pool):
    n = len(POOL_WINDOWS)
    eye = jnp.eye(n, dtype=w_pool.dtype)
    return jnp.einsum('gce,gh->gche', w_pool, eye).reshape(POOL_WIDTH, POOL_WIDTH)


def _router_weights(w_rg, b_rg, w_re, b_re):
    d = w_rg.shape[0]
    w_e = jnp.transpose(w_re, (1, 0, 2)).reshape(d, N_EXPERTS)
    w = jnp.concatenate([w_rg, w_e], axis=1)
    w = jnp.pad(w, ((0, 0), (0, _ROUTER_LANES - w.shape[1])))
    bias = jnp.concatenate([b_rg, b_re.reshape(N_EXPERTS)])
    bias = jnp.pad(bias, (0, _ROUTER_LANES - bias.shape[0])).reshape(1, _ROUTER_LANES)
    w_hi = w.astype(_BF16)
    w_lo = (w - w_hi.astype(_F32)).astype(_BF16)
    return jnp.stack([w_hi, w_lo]), bias


def _row_tile(n, target):
    return target if n % target == 0 else n


def kernel(x_prompt, x_sample, cache_k, cache_v, state_pool, w_in, w_out, w_pool, b_pool,
           pool_scale, ln1_g, ln1_b, w_rg, b_rg, w_re, b_re, w_gate, w_up, w_down, ln2_g, ln2_b):
    depth = w_in.shape[0]
    alpha = float((2 * depth) ** 0.25)
    bp, tp, d = x_prompt.shape
    bs, ts, _ = x_sample.shape
    kv_rows = cache_k.shape[2]
    assert kv_rows == MAX_WINDOW and tp % SUPER_BLOCK == 0 and ts == SUBLANES
    win_p = min(MAX_WINDOW, tp)

    xp = x_prompt.reshape(bp * tp, d)
    xs = x_sample.reshape(bs * ts, d)
    ckt = jnp.transpose(cache_k, (0, 1, 3, 4, 2))
    cvt = jnp.transpose(cache_v, (0, 1, 3, 4, 2))

    pk, pv, pu, sk, sv, su = [], [], [], [], [], []
    for l in range(depth):
        w_in_bf = w_in[l].astype(_BF16)
        w_out_bf = w_out[l].astype(_BF16)
        w_bd = _block_diag(w_pool[l]).astype(_BF16)
        b_pl = b_pool[l].reshape(1, POOL_WIDTH)
        sc_pl = pool_scale[l].reshape(1, POOL_WIDTH)
        w_router, b_router = _router_weights(w_rg[l], b_rg[l], w_re[l], b_re[l])
        g1, b1 = ln1_g[l].reshape(1, d), ln1_b[l].reshape(1, d)
        g2, b2 = ln2_g[l].reshape(1, d), ln2_b[l].reshape(1, d)

        kvu_p, qkv_p, qkv4_p, qkv16_p = _proj_prompt(xp, w_in_bf, bp, tp, 512)
        a_p = _attn_prompt(qkv_p, qkv4_p, qkv16_p, bp, tp)
        kvu_p3 = kvu_p.reshape(bp, tp, PROJ_WIDTH - ATTN_WIDTH)
        u_p = kvu_p3[:, :, 2 * ATTN_WIDTH:]
        ext_p = jnp.concatenate([jnp.zeros((bp, POOL_HALO, POOL_WIDTH), _F32), u_p], axis=1)
        p_p = _pool(ext_p, w_bd, b_pl, sc_pl, 0, 1, 128)
        h_p = _mix(a_p, p_p, xp, w_out_bf, g1, b1, alpha, _row_tile(bp * tp, 512))
        xp = _moe(h_p, w_router, b_router, w_gate, w_up, w_down, l, g2, b2, alpha,
                  _row_tile(bp * tp, 2048))
        pk.append(kvu_p3[:, tp - win_p:, 0:ATTN_WIDTH].reshape(bp, win_p, N_HEADS, HEAD_DIM))
        pv.append(kvu_p3[:, tp - win_p:, ATTN_WIDTH:2 * ATTN_WIDTH]
                  .reshape(bp, win_p, N_HEADS, HEAD_DIM))
        pu.append(u_p[:, tp - POOL_PREFIX:])

        proj_s = _proj_sample(xs, w_in_bf)
        a_s = _attn_sample(proj_s, ckt, cvt, l, bs, ts)
        proj_s3 = proj_s.reshape(bs, ts, PROJ_WIDTH)
        u_s = proj_s3[:, :, QKV_WIDTH:]
        ext_s = jnp.concatenate(
            [jnp.zeros((bs, POOL_HALO - POOL_PREFIX, POOL_WIDTH), _F32), state_pool[l], u_s],
            axis=1)
        p_s = _pool(ext_s, w_bd, b_pl, sc_pl, PAST_LEN, bs, ts)
        h_s = _mix(a_s.astype(_BF16), p_s, xs, w_out_bf, g1, b1, alpha, bs * ts)
        xs = _moe(h_s, w_router, b_router, w_gate, w_up, w_down, l, g2, b2, alpha, bs * ts)
        sk.append(proj_s3[:, :, ATTN_WIDTH:2 * ATTN_WIDTH].reshape(bs, ts, N_HEADS, HEAD_DIM))
        sv.append(proj_s3[:, :, 2 * ATTN_WIDTH:QKV_WIDTH].reshape(bs, ts, N_HEADS, HEAD_DIM))
        su.append(u_s)

    return (xp.reshape(bp, tp, d), xs.reshape(bs, ts, d), jnp.stack(pk), jnp.stack(pv),
            jnp.stack(pu), jnp.stack(sk), jnp.stack(sv), jnp.stack(su))
```

```python
import functools
import math

import jax
import jax.numpy as jnp
from jax import lax
from jax.experimental import pallas as pl
from jax.experimental.pallas import tpu as pltpu

N_HEADS = 12
HEAD_DIM = 64
ATTN_WIDTH = N_HEADS * HEAD_DIM
POOL_WIDTH = 256
PROJ_WIDTH = 3 * ATTN_WIDTH + POOL_WIDTH
QKV_WIDTH = 3 * ATTN_WIDTH
POOL_WINDOWS = (2, 4, 8, 16)
POOL_GROUP = POOL_WIDTH // len(POOL_WINDOWS)
POOL_PREFIX = max(POOL_WINDOWS) - 1
POOL_HALO = 16
DILATIONS = (1, 4, 16)
WINDOW_STEPS = 128
MAX_WINDOW = 2048
PAST_LEN = 8192
N_EXPERT_GROUPS = 4
EXPERTS_PER_GROUP = 4
N_EXPERTS = 16
D_EXPERT = 256
LN_EPS = 1e-5
NEG_INF = -1e30
QK_SCALE = HEAD_DIM ** -0.5
LOG2_E = math.log2(math.e)

LANES = 128
SUBLANES = 8
HEADS_PER_TILE = LANES // HEAD_DIM
N_HEAD_TILES = ATTN_WIDTH // LANES
MXU_WIDTH = 256

ATTN_BLOCK = WINDOW_STEPS
SUPER_BLOCK = ATTN_BLOCK * max(DILATIONS)
BLOCKS_PER_SUPER = SUPER_BLOCK // ATTN_BLOCK

_F32 = jnp.float32
_BF16 = jnp.bfloat16
_NT = (((1,), (1,)), ((), ()))


def _params(n_grid_dims, vmem_mib):
    return pltpu.CompilerParams(
        dimension_semantics=("arbitrary",) * n_grid_dims,
        vmem_limit_bytes=vmem_mib * 1024 * 1024)


def _layer_norm(z, g, b):
    mu = jnp.mean(z, axis=-1, keepdims=True)
    zc = z - mu
    var = jnp.mean(zc * zc, axis=-1, keepdims=True)
    return zc * lax.rsqrt(var + LN_EPS) * g + b


def _proj_sample_kernel(x_ref, w_ref, of_ref):
    xb = x_ref[...].astype(_BF16)
    for j in range(0, PROJ_WIDTH, MXU_WIDTH):
        of_ref[:, j:j + MXU_WIDTH] = jnp.dot(xb, w_ref[:, j:j + MXU_WIDTH],
                                             preferred_element_type=_F32)


def _proj_sample(x, w_in_bf):
    n, d = x.shape
    return pl.pallas_call(
        _proj_sample_kernel,
        grid=(1,),
        in_specs=[pl.BlockSpec((n, d), lambda i: (0, 0)),
                  pl.BlockSpec((d, PROJ_WIDTH), lambda i: (0, 0))],
        out_specs=pl.BlockSpec((n, PROJ_WIDTH), lambda i: (0, 0)),
        out_shape=jax.ShapeDtypeStruct((n, PROJ_WIDTH), _F32),
        compiler_params=_params(1, 48),
        name="proj_sample",
    )(x, w_in_bf)


def _proj_prompt_kernel(x_ref, w_ref, kv_ref, u_ref, ob_ref, o4_ref, o16_ref,
                        slab_scr, cls_scr):
    tm = x_ref.shape[0]
    xb = x_ref[...].astype(_BF16)
    n_slab = 0
    for j in range(0, PROJ_WIDTH, MXU_WIDTH):
        r = jnp.dot(xb, w_ref[:, j:j + MXU_WIDTH], preferred_element_type=_F32)
        if j >= QKV_WIDTH:
            u_ref[...] = r
            continue
        if j >= ATTN_WIDTH:
            kv_ref[:, j - ATTN_WIDTH:j - ATTN_WIDTH + MXU_WIDTH] = r
        if j < ATTN_WIDTH:
            r = r * (QK_SCALE * LOG2_E)
        ob_ref[:, j:j + MXU_WIDTH] = r.astype(_BF16)
        for half in range(MXU_WIDTH // LANES):
            lanes = slice(j + half * LANES, j + (half + 1) * LANES)
            buf = n_slab % 2
            n_slab += 1
            slab_scr[buf] = r[:, half * LANES:(half + 1) * LANES]
            for c4 in range(4):
                x4 = slab_scr[buf, pl.ds(c4, tm // 4, stride=4), :]
                o4_ref[c4, :, lanes] = x4.astype(_BF16)
                cls_scr[buf, c4] = x4
                for c in range(4):
                    x16 = cls_scr[buf, c4, pl.ds(c, tm // 16, stride=4), :]
                    o16_ref[4 * c + c4, :, lanes] = x16.astype(_BF16)


def _proj_prompt(x, w_in_bf, batch, seq, win, tm):
    n, d = x.shape
    per_batch = seq // tm
    win_tiles = win // tm
    first_win = per_batch - win_tiles

    def cls(r):
        return pl.BlockSpec((r, tm // r, QKV_WIDTH),
                            lambda i: (i // per_batch, i % per_batch, 0))

    def kv_block(i):
        return ((i // per_batch) * win_tiles
                + jnp.maximum(i % per_batch - first_win, 0), 0)
    return pl.pallas_call(
        _proj_prompt_kernel,
        grid=(n // tm,),
        in_specs=[pl.BlockSpec((tm, d), lambda i: (i, 0)),
                  pl.BlockSpec((d, PROJ_WIDTH), lambda i: (0, 0))],
        out_specs=[pl.BlockSpec((tm, 2 * ATTN_WIDTH), kv_block),
                   pl.BlockSpec((tm, POOL_WIDTH), lambda i: (i, 0)),
                   pl.BlockSpec((tm, QKV_WIDTH), lambda i: (i, 0)),
                   cls(4), cls(16)],
        out_shape=[jax.ShapeDtypeStruct((batch * win, 2 * ATTN_WIDTH), _F32),
                   jax.ShapeDtypeStruct((n, POOL_WIDTH), _F32),
                   jax.ShapeDtypeStruct((n, QKV_WIDTH), _BF16),
                   jax.ShapeDtypeStruct((batch * 4, seq // 4, QKV_WIDTH), _BF16),
                   jax.ShapeDtypeStruct((batch * 16, seq // 16, QKV_WIDTH), _BF16)],
        scratch_shapes=[pltpu.VMEM((2, tm, LANES), _F32),
                        pltpu.VMEM((2, 4, tm // 4, LANES), _F32)],
        compiler_params=_params(1, 48),
        name="proj_prompt",
    )(x, w_in_bf)


def _attn_prompt_kernel(q1, q4, q16, k1, k1h, k4, k4h, k16, k16h,
                        v1, v1h, v4, v4h, v16, v16h, out_ref,
                        o_scr, m_scr, l_scr, bias_scr):
    blk = ATTN_BLOCK
    first_super = pl.program_id(2) == 0
    ii = lax.broadcasted_iota(jnp.int32, (blk, 2 * blk), 0)
    jj = lax.broadcasted_iota(jnp.int32, (blk, 2 * blk), 1)
    band = (jj >= ii) & (jj <= ii + WINDOW_STEPS)
    lo = jnp.where(first_super, blk, 0)
    bias_scr[0] = jnp.where(band, 0.0, NEG_INF)
    bias_scr[1] = jnp.where(band & (jj >= lo), 0.0, NEG_INF)
    head0 = lax.broadcasted_iota(jnp.int32, (blk, LANES), 1) < HEAD_DIM

    def block(g, r, q_ref, k_ref, kh_ref, v_ref, vh_ref, c, ib):
        off = ib * blk
        q = q_ref[c, off:off + blk, :]
        if ib == 0:
            bias = bias_scr[1]
            k2 = jnp.concatenate([kh_ref[c], k_ref[c, 0:blk, :]], axis=0)
            v2 = jnp.concatenate([vh_ref[c], v_ref[c, 0:blk, :]], axis=0)
        else:
            bias = bias_scr[0]
            k2 = k_ref[c, off - blk:off + blk, :]
            v2 = v_ref[c, off - blk:off + blk, :]
        zero = jnp.zeros_like(q)
        qq = jnp.concatenate([jnp.where(head0, q, zero), jnp.where(head0, zero, q)], axis=0)
        s = lax.dot_general(qq, k2, _NT, preferred_element_type=_F32)
        ms, ls, ps = [], [], []
        for h in range(HEADS_PER_TILE):
            sh = s[h * blk:(h + 1) * blk, :] + bias
            m = jnp.max(sh, axis=-1, keepdims=True)
            p = jnp.exp2(sh - m)
            ms.append(m)
            ls.append(jnp.sum(p, axis=-1, keepdims=True))
            ps.append(p.astype(_BF16))
        o = jnp.dot(jnp.concatenate(ps, axis=0), v2, preferred_element_type=_F32)
        if r == 1:
            rows = pl.ds(off, blk)
        else:
            rows = pl.ds(c + r * off, blk, stride=r)
        m_scr[g, rows, :] = jnp.where(head0, ms[0], ms[1])
        l_scr[g, rows, :] = jnp.where(head0, ls[0], ls[1])
        o_scr[g, rows, :] = jnp.where(head0, o[0:blk, :], o[blk:2 * blk, :])

    for g, (r, q_ref, k_ref, kh_ref, v_ref, vh_ref) in enumerate((
            (1, q1, k1, k1h, v1, v1h), (4, q4, k4, k4h, v4, v4h),
            (16, q16, k16, k16h, v16, v16h))):
        per_class = BLOCKS_PER_SUPER // r
        for c in range(r):
            for ib in range(per_class):
                block(g, r, q_ref, k_ref, kh_ref, v_ref, vh_ref, c, ib)

    chunk = 256

    def merge(ci, carry):
        rows = pl.ds(pl.multiple_of(ci * chunk, chunk), chunk)
        ms = [m_scr[g, rows, :] for g in range(3)]
        m = jnp.maximum(jnp.maximum(ms[0], ms[1]), ms[2])
        num = jnp.zeros((chunk, LANES), _F32)
        den = jnp.zeros((chunk, LANES), _F32)
        for g in range(3):
            a = jnp.exp2(ms[g] - m)
            num = num + a * o_scr[g, rows, :]
            den = den + a * l_scr[g, rows, :]
        out_ref[rows, :] = (num / den).astype(out_ref.dtype)
        return carry
    lax.fori_loop(0, SUPER_BLOCK // chunk, merge, 0)


def _attn_prompt(qkv, qkv4, qkv16, batch, seq):
    blk, sb = ATTN_BLOCK, SUPER_BLOCK
    n_super = seq // sb
    kcol, vcol = N_HEAD_TILES, 2 * N_HEAD_TILES
    views = {1: qkv.reshape(batch, seq, QKV_WIDTH), 4: qkv4, 16: qkv16}

    def main(r, col):
        return pl.BlockSpec((r, sb // r, LANES), lambda b, hp, s: (b, s, col + hp))

    def halo(r, col):
        per = sb // (r * blk)
        return pl.BlockSpec((r, blk, LANES),
                            lambda b, hp, s: (b, jnp.maximum(s * per - 1, 0), col + hp))

    in_specs = [main(r, 0) for r in DILATIONS]
    operands = [views[r] for r in DILATIONS]
    for col in (kcol, vcol):
        for r in DILATIONS:
            in_specs += [main(r, col), halo(r, col)]
            operands += [views[r], views[r]]
    return pl.pallas_call(
        _attn_prompt_kernel,
        grid=(batch, N_HEAD_TILES, n_super),
        in_specs=in_specs,
        out_specs=pl.BlockSpec((sb, LANES), lambda b, hp, s: (b * n_super + s, hp)),
        out_shape=jax.ShapeDtypeStruct((batch * seq, ATTN_WIDTH), _BF16),
        scratch_shapes=[
            pltpu.VMEM((3, sb, LANES), _F32),
            pltpu.VMEM((3, sb, LANES), _F32),
            pltpu.VMEM((3, sb, LANES), _F32),
            pltpu.VMEM((2, blk, 2 * blk), _F32)],
        compiler_params=_params(3, 48),
        name="attn_prompt",
    )(*operands)


def _branch_count(dist):
    return ((dist <= 128).astype(_F32)
            + (((dist & 3) == 0) & (dist <= 512)).astype(_F32)
            + (((dist & 15) == 0) & (dist <= MAX_WINDOW)).astype(_F32))


def _attn_sample_kernel(proj_ref, kt_ref, vt_ref, out_ref, *, n_new):
    rows = HEADS_PER_TILE * n_new
    t_c = lax.broadcasted_iota(jnp.int32, (rows, MAX_WINDOW), 0) & (n_new - 1)
    f_c = lax.broadcasted_iota(jnp.int32, (rows, MAX_WINDOW), 1)
    w_c = _branch_count(MAX_WINDOW + t_c - f_c)
    t_n = lax.broadcasted_iota(jnp.int32, (rows, LANES), 0) & (n_new - 1)
    f_n = lax.broadcasted_iota(jnp.int32, (rows, LANES), 1)
    d_n = t_n - f_n
    w_n = jnp.where((d_n >= 0) & (f_n < n_new), _branch_count(d_n), 0.0)
    head0 = lax.broadcasted_iota(jnp.int32, (n_new, LANES), 1) < HEAD_DIM
    pad = jnp.zeros((LANES - n_new, LANES), _F32)

    for hp in range(N_HEAD_TILES):
        sl = slice(hp * LANES, (hp + 1) * LANES)
        ksl = slice(ATTN_WIDTH + hp * LANES, ATTN_WIDTH + (hp + 1) * LANES)
        vsl = slice(2 * ATTN_WIDTH + hp * LANES, 2 * ATTN_WIDTH + (hp + 1) * LANES)
        heads = slice(hp * HEADS_PER_TILE, (hp + 1) * HEADS_PER_TILE)
        q = proj_ref[:, sl] * QK_SCALE
        q2 = jnp.concatenate([jnp.where(head0, q, 0.0),
                              jnp.where(head0, 0.0, q)], axis=0).astype(_BF16)
        kt = kt_ref[heads].reshape(LANES, MAX_WINDOW).astype(_BF16)
        vt = vt_ref[heads].reshape(LANES, MAX_WINDOW).astype(_BF16)
        k_n = jnp.concatenate([proj_ref[:, ksl], pad], axis=0).astype(_BF16)
        v_n = jnp.concatenate([proj_ref[:, vsl], pad], axis=0).astype(_BF16)
        s_c = jnp.dot(q2, kt, preferred_element_type=_F32)
        s_n = lax.dot_general(q2, k_n, _NT, preferred_element_type=_F32)
        s_c = jnp.where(w_c > 0.0, s_c, NEG_INF)
        s_n = jnp.where(w_n > 0.0, s_n, NEG_INF)
        m = jnp.maximum(jnp.max(s_c, axis=-1, keepdims=True),
                        jnp.max(s_n, axis=-1, keepdims=True))
        p_c = w_c * jnp.exp(s_c - m)
        p_n = w_n * jnp.exp(s_n - m)
        l = jnp.sum(p_c, axis=-1, keepdims=True) + jnp.sum(p_n, axis=-1, keepdims=True)
        o = (lax.dot_general(p_c.astype(_BF16), vt, _NT, preferred_element_type=_F32)
             + jnp.dot(p_n.astype(_BF16), v_n, preferred_element_type=_F32)) / l
        out_ref[:, sl] = jnp.where(head0, o[0:n_new, :], o[n_new:rows, :])


def _attn_sample(proj_f32, cache_kt, cache_vt, layer, batch, n_new):
    spec = pl.BlockSpec((None, None, N_HEADS, HEAD_DIM, MAX_WINDOW),
                        lambda b: (layer, b, 0, 0, 0))
    return pl.pallas_call(
        functools.partial(_attn_sample_kernel, n_new=n_new),
        grid=(batch,),
        in_specs=[pl.BlockSpec((n_new, PROJ_WIDTH), lambda b: (b, 0)), spec, spec],
        out_specs=pl.BlockSpec((n_new, ATTN_WIDTH), lambda b: (b, 0)),
        out_shape=jax.ShapeDtypeStruct((batch * n_new, ATTN_WIDTH), _F32),
        compiler_params=_params(1, 48),
        name="attn_sample",
    )(proj_f32, cache_kt, cache_vt)


def _pool_rows(ext_ref, base, pos_first, w_ref, b_ref, sc_ref, bb, chunk):
    n = bb * chunk
    lane = lax.broadcasted_iota(jnp.int32, (n, POOL_WIDTH), 1)
    window = jnp.where(lane < POOL_GROUP, 2,
                       jnp.where(lane < 2 * POOL_GROUP, 4,
                                 jnp.where(lane < 3 * POOL_GROUP, 8, 16)))
    row = lax.broadcasted_iota(jnp.int32, (bb, chunk, POOL_WIDTH), 1).reshape(n, POOL_WIDTH)

    def shifted(i):
        lo = base + POOL_HALO - i
        return ext_ref[:, lo:lo + chunk, :].reshape(n, POOL_WIDTH)
    cur = shifted(0)
    acc = cur + shifted(1)
    total = acc
    for g, w in enumerate(POOL_WINDOWS[1:], start=1):
        for i in range(w // 2, w):
            acc = acc + shifted(i)
        total = jnp.where(lane >= g * POOL_GROUP, acc, total)
    count = jnp.minimum(pos_first + row + 1, window).astype(_F32)
    diff = total / count - cur
    y = jnp.dot(diff.astype(_BF16), w_ref[...], preferred_element_type=_F32)
    return (y + b_ref[...]) * sc_ref[...]


def _pool_kernel(ext_ref, w_ref, b_ref, sc_ref, out_ref, *, bb, chunk, pos0):
    y = _pool_rows(ext_ref, 0, pos0, w_ref, b_ref, sc_ref, bb, chunk)
    out_ref[...] = y.astype(out_ref.dtype)


def _pool(ext, w_bd, bias, scale, pos0):
    batch, rows, _ = ext.shape
    seq = rows - POOL_HALO
    return pl.pallas_call(
        functools.partial(_pool_kernel, bb=batch, chunk=seq, pos0=pos0),
        grid=(1,),
        in_specs=[pl.BlockSpec((batch, rows, POOL_WIDTH), lambda i: (0, 0, 0)),
                  pl.BlockSpec((POOL_WIDTH, POOL_WIDTH), lambda i: (0, 0)),
                  pl.BlockSpec((1, POOL_WIDTH), lambda i: (0, 0)),
                  pl.BlockSpec((1, POOL_WIDTH), lambda i: (0, 0))],
        out_specs=pl.BlockSpec((batch * seq, POOL_WIDTH), lambda i: (0, 0)),
        out_shape=jax.ShapeDtypeStruct((batch * seq, POOL_WIDTH), _BF16),
        compiler_params=_params(1, 48),
        name="pool",
    )(ext, w_bd, bias, scale)


def _mix_kernel(a_ref, p_ref, x_ref, w_ref, g_ref, b_ref, out_ref, *, alpha):
    mix = (jnp.dot(a_ref[...], w_ref[0:ATTN_WIDTH, :], preferred_element_type=_F32)
           + jnp.dot(p_ref[...], w_ref[ATTN_WIDTH:, :], preferred_element_type=_F32))
    out_ref[...] = _layer_norm(alpha * x_ref[...] + mix, g_ref[...], b_ref[...])


def _mix(a, p, x, w_out_bf, g, b, alpha, tm):
    n, d = x.shape
    return pl.pallas_call(
        functools.partial(_mix_kernel, alpha=alpha),
        grid=(n // tm,),
        in_specs=[pl.BlockSpec((tm, ATTN_WIDTH), lambda i: (i, 0)),
                  pl.BlockSpec((tm, POOL_WIDTH), lambda i: (i, 0)),
                  pl.BlockSpec((tm, d), lambda i: (i, 0)),
                  pl.BlockSpec((ATTN_WIDTH + POOL_WIDTH, d), lambda i: (0, 0)),
                  pl.BlockSpec((1, d), lambda i: (0, 0)),
                  pl.BlockSpec((1, d), lambda i: (0, 0))],
        out_specs=pl.BlockSpec((tm, d), lambda i: (i, 0)),
        out_shape=jax.ShapeDtypeStruct((n, d), _F32),
        compiler_params=_params(1, 48),
        name="mix",
    )(a, p, x, w_out_bf, g, b)


_POOL_CHUNK = 128


def _mix_prompt_kernel(a_ref, u_ref, uh_ref, x_ref, w_ref, wp_ref, bp_ref, sp_ref,
                       g_ref, b_ref, out_ref, ext_scr, p_scr, *, alpha, tiles_per_seq):
    tm = x_ref.shape[0]
    ti = pl.program_id(0) % tiles_per_seq
    ext_scr[0, 0:POOL_HALO, :] = jnp.where(ti == 0, 0.0, uh_ref[...])
    ext_scr[0, POOL_HALO:, :] = u_ref[...]
    for r0 in range(0, tm, _POOL_CHUNK):
        y = _pool_rows(ext_scr, r0, ti * tm + r0, wp_ref, bp_ref, sp_ref, 1, _POOL_CHUNK)
        p_scr[r0:r0 + _POOL_CHUNK, :] = y.astype(_BF16)
    mix = (jnp.dot(a_ref[...], w_ref[0:ATTN_WIDTH, :], preferred_element_type=_F32)
           + jnp.dot(p_scr[...], w_ref[ATTN_WIDTH:, :], preferred_element_type=_F32))
    out_ref[...] = _layer_norm(alpha * x_ref[...] + mix, g_ref[...], b_ref[...])


def _mix_prompt(a, u, x, w_out_bf, w_bd, b_pool, s_pool, g, b, alpha, seq, tm):
    n, d = x.shape
    halo_per_tile = tm // POOL_HALO
    return pl.pallas_call(
        functools.partial(_mix_prompt_kernel, alpha=alpha, tiles_per_seq=seq // tm),
        grid=(n // tm,),
        in_specs=[pl.BlockSpec((tm, ATTN_WIDTH), lambda i: (i, 0)),
                  pl.BlockSpec((tm, POOL_WIDTH), lambda i: (i, 0)),
                  pl.BlockSpec((POOL_HALO, POOL_WIDTH),
                               lambda i: (jnp.maximum(i * halo_per_tile - 1, 0), 0)),
                  pl.BlockSpec((tm, d), lambda i: (i, 0)),
                  pl.BlockSpec((ATTN_WIDTH + POOL_WIDTH, d), lambda i: (0, 0)),
                  pl.BlockSpec((POOL_WIDTH, POOL_WIDTH), lambda i: (0, 0)),
                  pl.BlockSpec((1, POOL_WIDTH), lambda i: (0, 0)),
                  pl.BlockSpec((1, POOL_WIDTH), lambda i: (0, 0)),
                  pl.BlockSpec((1, d), lambda i: (0, 0)),
                  pl.BlockSpec((1, d), lambda i: (0, 0))],
        out_specs=pl.BlockSpec((tm, d), lambda i: (i, 0)),
        out_shape=jax.ShapeDtypeStruct((n, d), _F32),
        scratch_shapes=[pltpu.VMEM((1, tm + POOL_HALO, POOL_WIDTH), _F32),
                        pltpu.VMEM((tm, POOL_WIDTH), _BF16)],
        compiler_params=_params(1, 48),
        name="mix_prompt",
    )(a, u, u, x, w_out_bf, w_bd, b_pool, s_pool, g, b)


_ROUTER_LANES = LANES
_EXPERT_LANE0 = N_EXPERT_GROUPS


def _router_combine(h, hb, wr_ref, br_ref):
    h_lo = (h - hb.astype(_F32)).astype(_BF16)
    logits = (jnp.dot(hb, wr_ref[0], preferred_element_type=_F32)
              + jnp.dot(h_lo, wr_ref[0], preferred_element_type=_F32)
              + jnp.dot(hb, wr_ref[1], preferred_element_type=_F32)) + br_ref[...]
    lane = lax.broadcasted_iota(jnp.int32, logits.shape, 1)
    big = _ROUTER_LANES
    g_logit = jnp.where(lane < N_EXPERT_GROUPS, logits, NEG_INF)
    g_max = jnp.max(g_logit, axis=-1, keepdims=True)
    g_sel = jnp.min(jnp.where(g_logit == g_max, lane, big), axis=-1, keepdims=True)
    p_group = 1.0 / jnp.sum(jnp.exp(g_logit - g_max), axis=-1, keepdims=True)
    lane_group = (lane - _EXPERT_LANE0) >> 2
    in_group = ((lane >= _EXPERT_LANE0) & (lane < _EXPERT_LANE0 + N_EXPERTS)
                & (lane_group == g_sel))
    e_logit = jnp.where(in_group, logits, NEG_INF)
    v1 = jnp.max(e_logit, axis=-1, keepdims=True)
    i1 = jnp.min(jnp.where(e_logit == v1, lane, big), axis=-1, keepdims=True)
    e_rest = jnp.where(lane == i1, NEG_INF, e_logit)
    v2 = jnp.max(e_rest, axis=-1, keepdims=True)
    i2 = jnp.min(jnp.where(e_rest == v2, lane, big), axis=-1, keepdims=True)
    t = jnp.exp(v2 - v1)
    w1 = 1.0 / (1.0 + t)
    w2 = t / (1.0 + t)
    return p_group * (jnp.where(lane == i1, w1, 0.0) + jnp.where(lane == i2, w2, 0.0))


def _moe_kernel(h_ref, wr_ref, br_ref, wg_ref, wu_ref, wd_ref, g_ref, b_ref, out_ref,
                hb_scr, comb_scr, *, alpha):
    e = pl.program_id(1)

    @pl.when(e == 0)
    def _():
        h = h_ref[...]
        hb = h.astype(_BF16)
        hb_scr[...] = hb
        comb_scr[...] = _router_combine(h, hb, wr_ref, br_ref)
        out_ref[...] = jnp.zeros_like(out_ref)

    comb = comb_scr[...]
    lane = lax.broadcasted_iota(jnp.int32, comb.shape, 1)
    gate = jnp.sum(jnp.where(lane == _EXPERT_LANE0 + e, comb, 0.0), axis=-1, keepdims=True)
    hb = hb_scr[...]
    hg = jnp.dot(hb, wg_ref[...].astype(_BF16), preferred_element_type=_F32)
    hu = jnp.dot(hb, wu_ref[...].astype(_BF16), preferred_element_type=_F32)
    act = hg * (1.0 / (1.0 + jnp.exp(-hg))) * hu * gate
    out_ref[...] += jnp.dot(act.astype(_BF16), wd_ref[...].astype(_BF16),
                            preferred_element_type=_F32)

    @pl.when(e == N_EXPERTS - 1)
    def _():
        out_ref[...] = _layer_norm(alpha * h_ref[...] + out_ref[...], g_ref[...], b_ref[...])


def _moe(h, w_router, b_router, w_gate, w_up, w_down, layer, g, b, alpha, tm):
    n, d = h.shape
    return pl.pallas_call(
        functools.partial(_moe_kernel, alpha=alpha),
        grid=(n // tm, N_EXPERTS),
        in_specs=[pl.BlockSpec((tm, d), lambda i, e: (i, 0)),
                  pl.BlockSpec((2, d, _ROUTER_LANES), lambda i, e: (0, 0, 0)),
                  pl.BlockSpec((1, _ROUTER_LANES), lambda i, e: (0, 0)),
                  pl.BlockSpec((None, None, d, D_EXPERT), lambda i, e: (layer, e, 0, 0)),
                  pl.BlockSpec((None, None, d, D_EXPERT), lambda i, e: (layer, e, 0, 0)),
                  pl.BlockSpec((None, None, D_EXPERT, d), lambda i, e: (layer, e, 0, 0)),
                  pl.BlockSpec((1, d), lambda i, e: (0, 0)),
                  pl.BlockSpec((1, d), lambda i, e: (0, 0))],
        out_specs=pl.BlockSpec((tm, d), lambda i, e: (i, 0)),
        out_shape=jax.ShapeDtypeStruct((n, d), _F32),
        scratch_shapes=[pltpu.VMEM((tm, d), _BF16),
                        pltpu.VMEM((tm, _ROUTER_LANES), _F32)],
        compiler_params=_params(2, 60),
        name="moe",
    )(h, w_router, b_router, w_gate, w_up, w_down, g, b)


def _block_diag(w_pool):
    n = len(POOL_WINDOWS)
    eye = jnp.eye(n, dtype=w_pool.dtype)
    return jnp.einsum('gce,gh->gche', w_pool, eye).reshape(POOL_WIDTH, POOL_WIDTH)


def _router_weights(w_rg, b_rg, w_re, b_re):
    d = w_rg.shape[0]
    w_e = jnp.transpose(w_re, (1, 0, 2)).reshape(d, N_EXPERTS)
    w = jnp.concatenate([w_rg, w_e], axis=1)
    w = jnp.pad(w, ((0, 0), (0, _ROUTER_LANES - w.shape[1])))
    bias = jnp.concatenate([b_rg, b_re.reshape(N_EXPERTS)])
    bias = jnp.pad(bias, (0, _ROUTER_LANES - bias.shape[0])).reshape(1, _ROUTER_LANES)
    w_hi = w.astype(_BF16)
    w_lo = (w - w_hi.astype(_F32)).astype(_BF16)
    return jnp.stack([w_hi, w_lo]), bias


def _row_tile(n, target):
    return target if n % target == 0 else n


def kernel(x_prompt, x_sample, cache_k, cache_v, state_pool, w_in, w_out, w_pool, b_pool,
           pool_scale, ln1_g, ln1_b, w_rg, b_rg, w_re, b_re, w_gate, w_up, w_down, ln2_g, ln2_b):
    depth = w_in.shape[0]
    alpha = float((2 * depth) ** 0.25)
    bp, tp, d = x_prompt.shape
    bs, ts, _ = x_sample.shape
    kv_rows = cache_k.shape[2]
    assert kv_rows == MAX_WINDOW and tp % SUPER_BLOCK == 0 and ts == SUBLANES
    win_p = min(MAX_WINDOW, tp)

    xp = x_prompt.reshape(bp * tp, d)
    xs = x_sample.reshape(bs * ts, d)
    ckt = jnp.transpose(cache_k, (0, 1, 3, 4, 2))
    cvt = jnp.transpose(cache_v, (0, 1, 3, 4, 2))

    pk, pv, pu, sk, sv, su = [], [], [], [], [], []
    for l in range(depth):
        w_in_bf = w_in[l].astype(_BF16)
        w_out_bf = w_out[l].astype(_BF16)
        w_bd = _block_diag(w_pool[l]).astype(_BF16)
        b_pl = b_pool[l].reshape(1, POOL_WIDTH)
        sc_pl = pool_scale[l].reshape(1, POOL_WIDTH)
        w_router, b_router = _router_weights(w_rg[l], b_rg[l], w_re[l], b_re[l])
        g1, b1 = ln1_g[l].reshape(1, d), ln1_b[l].reshape(1, d)
        g2, b2 = ln2_g[l].reshape(1, d), ln2_b[l].reshape(1, d)

        kv_p, u_p, qkv_p, qkv4_p, qkv16_p = _proj_prompt(xp, w_in_bf, bp, tp, win_p, 512)
        a_p = _attn_prompt(qkv_p, qkv4_p, qkv16_p, bp, tp)
        h_p = _mix_prompt(a_p, u_p, xp, w_out_bf, w_bd, b_pl, sc_pl, g1, b1, alpha, tp, 512)
        xp = _moe(h_p, w_router, b_router, w_gate, w_up, w_down, l, g2, b2, alpha,
                  _row_tile(bp * tp, 2048))
        kv_p3 = kv_p.reshape(bp, win_p, 2 * ATTN_WIDTH)
        pk.append(kv_p3[:, :, 0:ATTN_WIDTH].reshape(bp, win_p, N_HEADS, HEAD_DIM))
        pv.append(kv_p3[:, :, ATTN_WIDTH:].reshape(bp, win_p, N_HEADS, HEAD_DIM))
        pu.append(u_p.reshape(bp, tp, POOL_WIDTH)[:, tp - POOL_PREFIX:])

        proj_s = _proj_sample(xs, w_in_bf)
        a_s = _attn_sample(proj_s, ckt, cvt, l, bs, ts)
        proj_s3 = proj_s.reshape(bs, ts, PROJ_WIDTH)
        u_s = proj_s3[:, :, QKV_WIDTH:]
        ext_s = jnp.concatenate(
            [jnp.zeros((bs, POOL_HALO - POOL_PREFIX, POOL_WIDTH), _F32), state_pool[l], u_s],
            axis=1)
        p_s = _pool(ext_s, w_bd, b_pl, sc_pl, PAST_LEN)
        h_s = _mix(a_s.astype(_BF16), p_s, xs, w_out_bf, g1, b1, alpha, bs * ts)
        xs = _moe(h_s, w_router, b_router, w_gate, w_up, w_down, l, g2, b2, alpha, bs * ts)
        sk.append(proj_s3[:, :, ATTN_WIDTH:2 * ATTN_WIDTH].reshape(bs, ts, N_HEADS, HEAD_DIM))
        sv.append(proj_s3[:, :, 2 * ATTN_WIDTH:QKV_WIDTH].reshape(bs, ts, N_HEADS, HEAD_DIM))
        su.append(u_s)

    return (xp.reshape(bp, tp, d), xs.reshape(bs, ts, d), jnp.stack(pk), jnp.stack(pv),
            jnp.stack(pu), jnp.stack(sk), jnp.stack(sv), jnp.stack(su))
```

```python
import functools
import math

import jax
import jax.numpy as jnp
from jax import lax
from jax.experimental import pallas as pl
from jax.experimental.pallas import tpu as pltpu

N_HEADS = 12
HEAD_DIM = 64
ATTN_WIDTH = N_HEADS * HEAD_DIM
POOL_WIDTH = 256
PROJ_WIDTH = 3 * ATTN_WIDTH + POOL_WIDTH
QKV_WIDTH = 3 * ATTN_WIDTH
POOL_WINDOWS = (2, 4, 8, 16)
POOL_GROUP = POOL_WIDTH // len(POOL_WINDOWS)
POOL_PREFIX = max(POOL_WINDOWS) - 1
POOL_HALO = 16
DILATIONS = (1, 4, 16)
WINDOW_STEPS = 128
MAX_WINDOW = 2048
PAST_LEN = 8192
N_EXPERT_GROUPS = 4
EXPERTS_PER_GROUP = 4
N_EXPERTS = 16
D_EXPERT = 256
LN_EPS = 1e-5
NEG_INF = -1e30
QK_SCALE = HEAD_DIM ** -0.5
LOG2_E = math.log2(math.e)

LANES = 128
SUBLANES = 8
HEADS_PER_TILE = LANES // HEAD_DIM
N_HEAD_TILES = ATTN_WIDTH // LANES
MXU_WIDTH = 256

ATTN_BLOCK = WINDOW_STEPS
SUPER_BLOCK = ATTN_BLOCK * max(DILATIONS)
BLOCKS_PER_SUPER = SUPER_BLOCK // ATTN_BLOCK

_F32 = jnp.float32
_BF16 = jnp.bfloat16
_NT = (((1,), (1,)), ((), ()))


def _params(n_grid_dims, vmem_mib):
    return pltpu.CompilerParams(
        dimension_semantics=("arbitrary",) * n_grid_dims,
        vmem_limit_bytes=vmem_mib * 1024 * 1024)


def _layer_norm(z, g, b):
    mu = jnp.mean(z, axis=-1, keepdims=True)
    zc = z - mu
    var = jnp.mean(zc * zc, axis=-1, keepdims=True)
    return zc * lax.rsqrt(var + LN_EPS) * g + b


def _proj_sample_kernel(x_ref, w_ref, of_ref):
    xb = x_ref[...].astype(_BF16)
    for j in range(0, PROJ_WIDTH, MXU_WIDTH):
        of_ref[:, j:j + MXU_WIDTH] = jnp.dot(xb, w_ref[:, j:j + MXU_WIDTH],
                                             preferred_element_type=_F32)


def _proj_sample(x, w_in_bf):
    n, d = x.shape
    return pl.pallas_call(
        _proj_sample_kernel,
        grid=(1,),
        in_specs=[pl.BlockSpec((n, d), lambda i: (0, 0)),
                  pl.BlockSpec((d, PROJ_WIDTH), lambda i: (0, 0))],
        out_specs=pl.BlockSpec((n, PROJ_WIDTH), lambda i: (0, 0)),
        out_shape=jax.ShapeDtypeStruct((n, PROJ_WIDTH), _F32),
        compiler_params=_params(1, 48),
        name="proj_sample",
    )(x, w_in_bf)


def _proj_prompt_kernel(x_ref, w_ref, kv_ref, u_ref, ob_ref, o4_ref, o16_ref,
                        slab_scr, cls_scr):
    tm = x_ref.shape[0]
    xb = x_ref[...].astype(_BF16)
    n_slab = 0
    for j in range(0, PROJ_WIDTH, MXU_WIDTH):
        r = jnp.dot(xb, w_ref[:, j:j + MXU_WIDTH], preferred_element_type=_F32)
        if j >= QKV_WIDTH:
            u_ref[...] = r
            continue
        if j >= ATTN_WIDTH:
            kv_ref[:, j - ATTN_WIDTH:j - ATTN_WIDTH + MXU_WIDTH] = r
        if j < ATTN_WIDTH:
            r = r * (QK_SCALE * LOG2_E)
        ob_ref[:, j:j + MXU_WIDTH] = r.astype(_BF16)
        for half in range(MXU_WIDTH // LANES):
            lanes = slice(j + half * LANES, j + (half + 1) * LANES)
            buf = n_slab % 2
            n_slab += 1
            slab_scr[buf] = r[:, half * LANES:(half + 1) * LANES]
            for c4 in range(4):
                x4 = slab_scr[buf, pl.ds(c4, tm // 4, stride=4), :]
                o4_ref[c4, :, lanes] = x4.astype(_BF16)
                cls_scr[buf, c4] = x4
                for c in range(4):
                    x16 = cls_scr[buf, c4, pl.ds(c, tm // 16, stride=4), :]
                    o16_ref[4 * c + c4, :, lanes] = x16.astype(_BF16)


def _proj_prompt(x, w_in_bf, batch, seq, win, tm):
    n, d = x.shape
    per_batch = seq // tm
    win_tiles = win // tm
    first_win = per_batch - win_tiles

    def cls(r):
        return pl.BlockSpec((r, tm // r, QKV_WIDTH),
                            lambda i: (i // per_batch, i % per_batch, 0))

    def kv_block(i):
        return ((i // per_batch) * win_tiles
                + jnp.maximum(i % per_batch - first_win, 0), 0)
    return pl.pallas_call(
        _proj_prompt_kernel,
        grid=(n // tm,),
        in_specs=[pl.BlockSpec((tm, d), lambda i: (i, 0)),
                  pl.BlockSpec((d, PROJ_WIDTH), lambda i: (0, 0))],
        out_specs=[pl.BlockSpec((tm, 2 * ATTN_WIDTH), kv_block),
                   pl.BlockSpec((tm, POOL_WIDTH), lambda i: (i, 0)),
                   pl.BlockSpec((tm, QKV_WIDTH), lambda i: (i, 0)),
                   cls(4), cls(16)],
        out_shape=[jax.ShapeDtypeStruct((batch * win, 2 * ATTN_WIDTH), _F32),
                   jax.ShapeDtypeStruct((n, POOL_WIDTH), _F32),
                   jax.ShapeDtypeStruct((n, QKV_WIDTH), _BF16),
                   jax.ShapeDtypeStruct((batch * 4, seq // 4, QKV_WIDTH), _BF16),
                   jax.ShapeDtypeStruct((batch * 16, seq // 16, QKV_WIDTH), _BF16)],
        scratch_shapes=[pltpu.VMEM((2, tm, LANES), _F32),
                        pltpu.VMEM((2, 4, tm // 4, LANES), _F32)],
        compiler_params=_params(1, 48),
        name="proj_prompt",
    )(x, w_in_bf)


def _attn_prompt_kernel(q1, q4, q16, k1, k1h, k4, k4h, k16, k16h,
                        v1, v1h, v4, v4h, v16, v16h, out_ref,
                        o_scr, m_scr, l_scr, bias_scr):
    blk = ATTN_BLOCK
    first_super = pl.program_id(2) == 0
    ii = lax.broadcasted_iota(jnp.int32, (blk, 2 * blk), 0)
    jj = lax.broadcasted_iota(jnp.int32, (blk, 2 * blk), 1)
    band = (jj >= ii) & (jj <= ii + WINDOW_STEPS)
    lo = jnp.where(first_super, blk, 0)
    bias_scr[0] = jnp.where(band, 0.0, NEG_INF)
    bias_scr[1] = jnp.where(band & (jj >= lo), 0.0, NEG_INF)
    head0 = lax.broadcasted_iota(jnp.int32, (blk, LANES), 1) < HEAD_DIM

    def block(g, r, q_ref, k_ref, kh_ref, v_ref, vh_ref, c, ib):
        off = ib * blk
        q = q_ref[c, off:off + blk, :]
        if ib == 0:
            bias = bias_scr[1]
            k2 = jnp.concatenate([kh_ref[c], k_ref[c, 0:blk, :]], axis=0)
            v2 = jnp.concatenate([vh_ref[c], v_ref[c, 0:blk, :]], axis=0)
        else:
            bias = bias_scr[0]
            k2 = k_ref[c, off - blk:off + blk, :]
            v2 = v_ref[c, off - blk:off + blk, :]
        zero = jnp.zeros_like(q)
        qq = jnp.concatenate([jnp.where(head0, q, zero), jnp.where(head0, zero, q)], axis=0)
        s = lax.dot_general(qq, k2, _NT, preferred_element_type=_F32)
        ms, ls, ps = [], [], []
        for h in range(HEADS_PER_TILE):
            sh = s[h * blk:(h + 1) * blk, :] + bias
            m = jnp.max(sh, axis=-1, keepdims=True)
            p = jnp.exp2(sh - m)
            ms.append(m)
            ls.append(jnp.sum(p, axis=-1, keepdims=True))
            ps.append(p.astype(_BF16))
        o = jnp.dot(jnp.concatenate(ps, axis=0), v2, preferred_element_type=_F32)
        if r == 1:
            rows = pl.ds(off, blk)
        else:
            rows = pl.ds(c + r * off, blk, stride=r)
        m_scr[g, rows, :] = jnp.where(head0, ms[0], ms[1])
        l_scr[g, rows, :] = jnp.where(head0, ls[0], ls[1])
        o_scr[g, rows, :] = jnp.where(head0, o[0:blk, :], o[blk:2 * blk, :])

    for g, (r, q_ref, k_ref, kh_ref, v_ref, vh_ref) in enumerate((
            (1, q1, k1, k1h, v1, v1h), (4, q4, k4, k4h, v4, v4h),
            (16, q16, k16, k16h, v16, v16h))):
        per_class = BLOCKS_PER_SUPER // r
        for c in range(r):
            for ib in range(per_class):
                block(g, r, q_ref, k_ref, kh_ref, v_ref, vh_ref, c, ib)

    chunk = 256

    def merge(ci, carry):
        rows = pl.ds(pl.multiple_of(ci * chunk, chunk), chunk)
        ms = [m_scr[g, rows, :] for g in range(3)]
        m = jnp.maximum(jnp.maximum(ms[0], ms[1]), ms[2])
        num = jnp.zeros((chunk, LANES), _F32)
        den = jnp.zeros((chunk, LANES), _F32)
        for g in range(3):
            a = jnp.exp2(ms[g] - m)
            num = num + a * o_scr[g, rows, :]
            den = den + a * l_scr[g, rows, :]
        out_ref[rows, :] = (num / den).astype(out_ref.dtype)
        return carry
    lax.fori_loop(0, SUPER_BLOCK // chunk, merge, 0)


def _attn_prompt(qkv, qkv4, qkv16, batch, seq):
    blk, sb = ATTN_BLOCK, SUPER_BLOCK
    n_super = seq // sb
    kcol, vcol = N_HEAD_TILES, 2 * N_HEAD_TILES
    views = {1: qkv.reshape(batch, seq, QKV_WIDTH), 4: qkv4, 16: qkv16}

    def main(r, col):
        return pl.BlockSpec((r, sb // r, LANES), lambda b, hp, s: (b, s, col + hp))

    def halo(r, col):
        per = sb // (r * blk)
        return pl.BlockSpec((r, blk, LANES),
                            lambda b, hp, s: (b, jnp.maximum(s * per - 1, 0), col + hp))

    in_specs = [main(r, 0) for r in DILATIONS]
    operands = [views[r] for r in DILATIONS]
    for col in (kcol, vcol):
        for r in DILATIONS:
            in_specs += [main(r, col), halo(r, col)]
            operands += [views[r], views[r]]
    return pl.pallas_call(
        _attn_prompt_kernel,
        grid=(batch, N_HEAD_TILES, n_super),
        in_specs=in_specs,
        out_specs=pl.BlockSpec((sb, LANES), lambda b, hp, s: (b * n_super + s, hp)),
        out_shape=jax.ShapeDtypeStruct((batch * seq, ATTN_WIDTH), _BF16),
        scratch_shapes=[
            pltpu.VMEM((3, sb, LANES), _F32),
            pltpu.VMEM((3, sb, LANES), _F32),
            pltpu.VMEM((3, sb, LANES), _F32),
            pltpu.VMEM((2, blk, 2 * blk), _F32)],
        compiler_params=_params(3, 48),
        name="attn_prompt",
    )(*operands)


def _branch_count(dist):
    return ((dist <= 128).astype(_F32)
            + (((dist & 3) == 0) & (dist <= 512)).astype(_F32)
            + (((dist & 15) == 0) & (dist <= MAX_WINDOW)).astype(_F32))


def _attn_sample_kernel(proj_ref, kt_ref, vt_ref, out_ref, *, n_new):
    rows = HEADS_PER_TILE * n_new
    t_c = lax.broadcasted_iota(jnp.int32, (rows, MAX_WINDOW), 0) & (n_new - 1)
    f_c = lax.broadcasted_iota(jnp.int32, (rows, MAX_WINDOW), 1)
    w_c = _branch_count(MAX_WINDOW + t_c - f_c)
    t_n = lax.broadcasted_iota(jnp.int32, (rows, LANES), 0) & (n_new - 1)
    f_n = lax.broadcasted_iota(jnp.int32, (rows, LANES), 1)
    d_n = t_n - f_n
    w_n = jnp.where((d_n >= 0) & (f_n < n_new), _branch_count(d_n), 0.0)
    head0 = lax.broadcasted_iota(jnp.int32, (n_new, LANES), 1) < HEAD_DIM
    pad = jnp.zeros((LANES - n_new, LANES), _F32)

    for hp in range(N_HEAD_TILES):
        sl = slice(hp * LANES, (hp + 1) * LANES)
        ksl = slice(ATTN_WIDTH + hp * LANES, ATTN_WIDTH + (hp + 1) * LANES)
        vsl = slice(2 * ATTN_WIDTH + hp * LANES, 2 * ATTN_WIDTH + (hp + 1) * LANES)
        heads = slice(hp * HEADS_PER_TILE, (hp + 1) * HEADS_PER_TILE)
        q = proj_ref[:, sl] * QK_SCALE
        q2 = jnp.concatenate([jnp.where(head0, q, 0.0),
                              jnp.where(head0, 0.0, q)], axis=0).astype(_BF16)
        kt = kt_ref[heads].reshape(LANES, MAX_WINDOW).astype(_BF16)
        vt = vt_ref[heads].reshape(LANES, MAX_WINDOW).astype(_BF16)
        k_n = jnp.concatenate([proj_ref[:, ksl], pad], axis=0).astype(_BF16)
        v_n = jnp.concatenate([proj_ref[:, vsl], pad], axis=0).astype(_BF16)
        s_c = jnp.dot(q2, kt, preferred_element_type=_F32)
        s_n = lax.dot_general(q2, k_n, _NT, preferred_element_type=_F32)
        s_c = jnp.where(w_c > 0.0, s_c, NEG_INF)
        s_n = jnp.where(w_n > 0.0, s_n, NEG_INF)
        m = jnp.maximum(jnp.max(s_c, axis=-1, keepdims=True),
                        jnp.max(s_n, axis=-1, keepdims=True))
        p_c = w_c * jnp.exp(s_c - m)
        p_n = w_n * jnp.exp(s_n - m)
        l = jnp.sum(p_c, axis=-1, keepdims=True) + jnp.sum(p_n, axis=-1, keepdims=True)
        o = (lax.dot_general(p_c.astype(_BF16), vt, _NT, preferred_element_type=_F32)
             + jnp.dot(p_n.astype(_BF16), v_n, preferred_element_type=_F32)) / l
        out_ref[:, sl] = jnp.where(head0, o[0:n_new, :], o[n_new:rows, :])


def _attn_sample(proj_f32, cache_kt, cache_vt, layer, batch, n_new):
    spec = pl.BlockSpec((None, None, N_HEADS, HEAD_DIM, MAX_WINDOW),
                        lambda b: (layer, b, 0, 0, 0))
    return pl.pallas_call(
        functools.partial(_attn_sample_kernel, n_new=n_new),
        grid=(batch,),
        in_specs=[pl.BlockSpec((n_new, PROJ_WIDTH), lambda b: (b, 0)), spec, spec],
        out_specs=pl.BlockSpec((n_new, ATTN_WIDTH), lambda b: (b, 0)),
        out_shape=jax.ShapeDtypeStruct((batch * n_new, ATTN_WIDTH), _F32),
        compiler_params=_params(1, 48),
        name="attn_sample",
    )(proj_f32, cache_kt, cache_vt)


def _pool_rows(ext_ref, base, pos_first, w_ref, b_ref, sc_ref, bb, chunk):
    n = bb * chunk
    lane = lax.broadcasted_iota(jnp.int32, (n, POOL_WIDTH), 1)
    window = jnp.where(lane < POOL_GROUP, 2,
                       jnp.where(lane < 2 * POOL_GROUP, 4,
                                 jnp.where(lane < 3 * POOL_GROUP, 8, 16)))
    row = lax.broadcasted_iota(jnp.int32, (bb, chunk, POOL_WIDTH), 1).reshape(n, POOL_WIDTH)

    def shifted(i):
        lo = base + POOL_HALO - i
        return ext_ref[:, lo:lo + chunk, :].reshape(n, POOL_WIDTH)
    cur = shifted(0)
    acc = cur + shifted(1)
    total = acc
    for g, w in enumerate(POOL_WINDOWS[1:], start=1):
        for i in range(w // 2, w):
            acc = acc + shifted(i)
        total = jnp.where(lane >= g * POOL_GROUP, acc, total)
    count = jnp.minimum(pos_first + row + 1, window).astype(_F32)
    diff = total / count - cur
    y = jnp.dot(diff.astype(_BF16), w_ref[...], preferred_element_type=_F32)
    return (y + b_ref[...]) * sc_ref[...]


def _pool_kernel(ext_ref, w_ref, b_ref, sc_ref, out_ref, *, bb, chunk, pos0):
    y = _pool_rows(ext_ref, 0, pos0, w_ref, b_ref, sc_ref, bb, chunk)
    out_ref[...] = y.astype(out_ref.dtype)


def _pool(ext, w_bd, bias, scale, pos0):
    batch, rows, _ = ext.shape
    seq = rows - POOL_HALO
    return pl.pallas_call(
        functools.partial(_pool_kernel, bb=batch, chunk=seq, pos0=pos0),
        grid=(1,),
        in_specs=[pl.BlockSpec((batch, rows, POOL_WIDTH), lambda i: (0, 0, 0)),
                  pl.BlockSpec((POOL_WIDTH, POOL_WIDTH), lambda i: (0, 0)),
                  pl.BlockSpec((1, POOL_WIDTH), lambda i: (0, 0)),
                  pl.BlockSpec((1, POOL_WIDTH), lambda i: (0, 0))],
        out_specs=pl.BlockSpec((batch * seq, POOL_WIDTH), lambda i: (0, 0)),
        out_shape=jax.ShapeDtypeStruct((batch * seq, POOL_WIDTH), _BF16),
        compiler_params=_params(1, 48),
        name="pool",
    )(ext, w_bd, bias, scale)


def _mix_kernel(a_ref, p_ref, x_ref, w_ref, g_ref, b_ref, out_ref, *, alpha):
    mix = (jnp.dot(a_ref[...], w_ref[0:ATTN_WIDTH, :], preferred_element_type=_F32)
           + jnp.dot(p_ref[...], w_ref[ATTN_WIDTH:, :], preferred_element_type=_F32))
    out_ref[...] = _layer_norm(alpha * x_ref[...] + mix, g_ref[...], b_ref[...])


def _mix(a, p, x, w_out_bf, g, b, alpha, tm):
    n, d = x.shape
    return pl.pallas_call(
        functools.partial(_mix_kernel, alpha=alpha),
        grid=(n // tm,),
        in_specs=[pl.BlockSpec((tm, ATTN_WIDTH), lambda i: (i, 0)),
                  pl.BlockSpec((tm, POOL_WIDTH), lambda i: (i, 0)),
                  pl.BlockSpec((tm, d), lambda i: (i, 0)),
                  pl.BlockSpec((ATTN_WIDTH + POOL_WIDTH, d), lambda i: (0, 0)),
                  pl.BlockSpec((1, d), lambda i: (0, 0)),
                  pl.BlockSpec((1, d), lambda i: (0, 0))],
        out_specs=pl.BlockSpec((tm, d), lambda i: (i, 0)),
        out_shape=jax.ShapeDtypeStruct((n, d), _F32),
        compiler_params=_params(1, 48),
        name="mix",
    )(a, p, x, w_out_bf, g, b)


_POOL_CHUNK = 128


def _mix_prompt_kernel(a_ref, u_ref, uh_ref, x_ref, w_ref, wp_ref, bp_ref, sp_ref,
                       g_ref, b_ref, out_ref, ext_scr, p_scr, *, alpha, tiles_per_seq):
    tm = x_ref.shape[0]
    ti = pl.program_id(0) % tiles_per_seq
    ext_scr[0, 0:POOL_HALO, :] = jnp.where(ti == 0, 0.0, uh_ref[...])
    ext_scr[0, POOL_HALO:, :] = u_ref[...]
    for r0 in range(0, tm, _POOL_CHUNK):
        y = _pool_rows(ext_scr, r0, ti * tm + r0, wp_ref, bp_ref, sp_ref, 1, _POOL_CHUNK)
        p_scr[r0:r0 + _POOL_CHUNK, :] = y.astype(_BF16)
    mix = (jnp.dot(a_ref[...], w_ref[0:ATTN_WIDTH, :], preferred_element_type=_F32)
           + jnp.dot(p_scr[...], w_ref[ATTN_WIDTH:, :], preferred_element_type=_F32))
    out_ref[...] = _layer_norm(alpha * x_ref[...] + mix, g_ref[...], b_ref[...])


def _mix_prompt(a, u, x, w_out_bf, w_bd, b_pool, s_pool, g, b, alpha, seq, tm):
    n, d = x.shape
    halo_per_tile = tm // POOL_HALO
    return pl.pallas_call(
        functools.partial(_mix_prompt_kernel, alpha=alpha, tiles_per_seq=seq // tm),
        grid=(n // tm,),
        in_specs=[pl.BlockSpec((tm, ATTN_WIDTH), lambda i: (i, 0)),
                  pl.BlockSpec((tm, POOL_WIDTH), lambda i: (i, 0)),
                  pl.BlockSpec((POOL_HALO, POOL_WIDTH),
                               lambda i: (jnp.maximum(i * halo_per_tile - 1, 0), 0)),
                  pl.BlockSpec((tm, d), lambda i: (i, 0)),
                  pl.BlockSpec((ATTN_WIDTH + POOL_WIDTH, d), lambda i: (0, 0)),
                  pl.BlockSpec((POOL_WIDTH, POOL_WIDTH), lambda i: (0, 0)),
                  pl.BlockSpec((1, POOL_WIDTH), lambda i: (0, 0)),
                  pl.BlockSpec((1, POOL_WIDTH), lambda i: (0, 0)),
                  pl.BlockSpec((1, d), lambda i: (0, 0)),
                  pl.BlockSpec((1, d), lambda i: (0, 0))],
        out_specs=pl.BlockSpec((tm, d), lambda i: (i, 0)),
        out_shape=jax.ShapeDtypeStruct((n, d), _F32),
        scratch_shapes=[pltpu.VMEM((1, tm + POOL_HALO, POOL_WIDTH), _F32),
                        pltpu.VMEM((tm, POOL_WIDTH), _BF16)],
        compiler_params=_params(1, 48),
        name="mix_prompt",
    )(a, u, u, x, w_out_bf, w_bd, b_pool, s_pool, g, b)


_ROUTER_LANES = LANES
_EXPERT_LANE0 = N_EXPERT_GROUPS
_MOE_ROW_CHUNK = 256


def _router_combine(h, hb, wr_ref, br_ref):
    h_lo = (h - hb.astype(_F32)).astype(_BF16)
    hi_pass = jnp.dot(hb, wr_ref[...], preferred_element_type=_F32)
    lo_pass = jnp.dot(h_lo, wr_ref[:, 0:_ROUTER_LANES], preferred_element_type=_F32)
    logits = (hi_pass[:, 0:_ROUTER_LANES] + hi_pass[:, _ROUTER_LANES:]
              + lo_pass + br_ref[...])
    lane = lax.broadcasted_iota(jnp.int32, logits.shape, 1)
    big = _ROUTER_LANES
    g_logit = jnp.where(lane < N_EXPERT_GROUPS, logits, NEG_INF)
    g_max = jnp.max(g_logit, axis=-1, keepdims=True)
    g_sel = jnp.min(jnp.where(g_logit == g_max, lane, big), axis=-1, keepdims=True)
    p_group = 1.0 / jnp.sum(jnp.exp(g_logit - g_max), axis=-1, keepdims=True)
    lane_group = (lane - _EXPERT_LANE0) >> 2
    in_group = ((lane >= _EXPERT_LANE0) & (lane < _EXPERT_LANE0 + N_EXPERTS)
                & (lane_group == g_sel))
    e_logit = jnp.where(in_group, logits, NEG_INF)
    v1 = jnp.max(e_logit, axis=-1, keepdims=True)
    i1 = jnp.min(jnp.where(e_logit == v1, lane, big), axis=-1, keepdims=True)
    e_rest = jnp.where(lane == i1, NEG_INF, e_logit)
    v2 = jnp.max(e_rest, axis=-1, keepdims=True)
    i2 = jnp.min(jnp.where(e_rest == v2, lane, big), axis=-1, keepdims=True)
    t = jnp.exp(v2 - v1)
    w1 = 1.0 / (1.0 + t)
    w2 = t / (1.0 + t)
    return p_group * (jnp.where(lane == i1, w1, 0.0) + jnp.where(lane == i2, w2, 0.0))


def _moe_kernel(h_ref, wr_ref, br_ref, wg_ref, wu_ref, wd_ref, g_ref, b_ref, out_ref,
                hb_scr, comb_scr, *, alpha):
    e = pl.program_id(1)
    tm = h_ref.shape[0]
    chunk = min(tm, _MOE_ROW_CHUNK)

    @pl.when(e == 0)
    def _():
        for r0 in range(0, tm, chunk):
            h = h_ref[r0:r0 + chunk, :]
            hb = h.astype(_BF16)
            hb_scr[r0:r0 + chunk, :] = hb
            comb_scr[r0:r0 + chunk, :] = _router_combine(h, hb, wr_ref, br_ref)
        out_ref[...] = jnp.zeros_like(out_ref)

    comb = comb_scr[...]
    lane = lax.broadcasted_iota(jnp.int32, comb.shape, 1)
    gate = jnp.sum(jnp.where(lane == _EXPERT_LANE0 + e, comb, 0.0), axis=-1, keepdims=True)
    hb = hb_scr[...]
    hg = jnp.dot(hb, wg_ref[...].astype(_BF16), preferred_element_type=_F32)
    hu = jnp.dot(hb, wu_ref[...].astype(_BF16), preferred_element_type=_F32)
    act = hg * (1.0 / (1.0 + jnp.exp(-hg))) * hu * gate
    out_ref[...] += jnp.dot(act.astype(_BF16), wd_ref[...].astype(_BF16),
                            preferred_element_type=_F32)

    @pl.when(e == N_EXPERTS - 1)
    def _():
        out_ref[...] = _layer_norm(alpha * h_ref[...] + out_ref[...], g_ref[...], b_ref[...])


def _moe(h, w_router, b_router, w_gate, w_up, w_down, layer, g, b, alpha, tm):
    n, d = h.shape
    return pl.pallas_call(
        functools.partial(_moe_kernel, alpha=alpha),
        grid=(n // tm, N_EXPERTS),
        in_specs=[pl.BlockSpec((tm, d), lambda i, e: (i, 0)),
                  pl.BlockSpec((d, 2 * _ROUTER_LANES), lambda i, e: (0, 0)),
                  pl.BlockSpec((1, _ROUTER_LANES), lambda i, e: (0, 0)),
                  pl.BlockSpec((None, None, d, D_EXPERT), lambda i, e: (layer, e, 0, 0)),
                  pl.BlockSpec((None, None, d, D_EXPERT), lambda i, e: (layer, e, 0, 0)),
                  pl.BlockSpec((None, None, D_EXPERT, d), lambda i, e: (layer, e, 0, 0)),
                  pl.BlockSpec((1, d), lambda i, e: (0, 0)),
                  pl.BlockSpec((1, d), lambda i, e: (0, 0))],
        out_specs=pl.BlockSpec((tm, d), lambda i, e: (i, 0)),
        out_shape=jax.ShapeDtypeStruct((n, d), _F32),
        scratch_shapes=[pltpu.VMEM((tm, d), _BF16),
                        pltpu.VMEM((tm, _ROUTER_LANES), _F32)],
        compiler_params=_params(2, 60),
        name="moe",
    )(h, w_router, b_router, w_gate, w_up, w_down, g, b)


def _block_diag(w_pool):
    n = len(POOL_WINDOWS)
    eye = jnp.eye(n, dtype=w_pool.dtype)
    return jnp.einsum('gce,gh->gche', w_pool, eye).reshape(POOL_WIDTH, POOL_WIDTH)


def _router_weights(w_rg, b_rg, w_re, b_re):
    d = w_rg.shape[0]
    w_e = jnp.transpose(w_re, (1, 0, 2)).reshape(d, N_EXPERTS)
    w = jnp.concatenate([w_rg, w_e], axis=1)
    w = jnp.pad(w, ((0, 0), (0, _ROUTER_LANES - w.shape[1])))
    bias = jnp.concatenate([b_rg, b_re.reshape(N_EXPERTS)])
    bias = jnp.pad(bias, (0, _ROUTER_LANES - bias.shape[0])).reshape(1, _ROUTER_LANES)
    w_hi = w.astype(_BF16)
    w_lo = (w - w_hi.astype(_F32)).astype(_BF16)
    return jnp.concatenate([w_hi, w_lo], axis=1), bias


def _row_tile(n, target):
    return target if n % target == 0 else n


def kernel(x_prompt, x_sample, cache_k, cache_v, state_pool, w_in, w_out, w_pool, b_pool,
           pool_scale, ln1_g, ln1_b, w_rg, b_rg, w_re, b_re, w_gate, w_up, w_down, ln2_g, ln2_b):
    depth = w_in.shape[0]
    alpha = float((2 * depth) ** 0.25)
    bp, tp, d = x_prompt.shape
    bs, ts, _ = x_sample.shape
    kv_rows = cache_k.shape[2]
    assert kv_rows == MAX_WINDOW and tp % SUPER_BLOCK == 0 and ts == SUBLANES
    win_p = min(MAX_WINDOW, tp)

    xp = x_prompt.reshape(bp * tp, d)
    xs = x_sample.reshape(bs * ts, d)
    ckt = jnp.transpose(cache_k, (0, 1, 3, 4, 2))
    cvt = jnp.transpose(cache_v, (0, 1, 3, 4, 2))

    pk, pv, pu, sk, sv, su = [], [], [], [], [], []
    for l in range(depth):
        w_in_bf = w_in[l].astype(_BF16)
        w_out_bf = w_out[l].astype(_BF16)
        w_bd = _block_diag(w_pool[l]).astype(_BF16)
        b_pl = b_pool[l].reshape(1, POOL_WIDTH)
        sc_pl = pool_scale[l].reshape(1, POOL_WIDTH)
        w_router, b_router = _router_weights(w_rg[l], b_rg[l], w_re[l], b_re[l])
        g1, b1 = ln1_g[l].reshape(1, d), ln1_b[l].reshape(1, d)
        g2, b2 = ln2_g[l].reshape(1, d), ln2_b[l].reshape(1, d)

        kv_p, u_p, qkv_p, qkv4_p, qkv16_p = _proj_prompt(xp, w_in_bf, bp, tp, win_p, 512)
        a_p = _attn_prompt(qkv_p, qkv4_p, qkv16_p, bp, tp)
        h_p = _mix_prompt(a_p, u_p, xp, w_out_bf, w_bd, b_pl, sc_pl, g1, b1, alpha, tp, 512)
        xp = _moe(h_p, w_router, b_router, w_gate, w_up, w_down, l, g2, b2, alpha,
                  _row_tile(bp * tp, 2048))
        kv_p3 = kv_p.reshape(bp, win_p, 2 * ATTN_WIDTH)
        pk.append(kv_p3[:, :, 0:ATTN_WIDTH].reshape(bp, win_p, N_HEADS, HEAD_DIM))
        pv.append(kv_p3[:, :, ATTN_WIDTH:].reshape(bp, win_p, N_HEADS, HEAD_DIM))
        pu.append(u_p.reshape(bp, tp, POOL_WIDTH)[:, tp - POOL_PREFIX:])

        proj_s = _proj_sample(xs, w_in_bf)
        a_s = _attn_sample(proj_s, ckt, cvt, l, bs, ts)
        proj_s3 = proj_s.reshape(bs, ts, PROJ_WIDTH)
        u_s = proj_s3[:, :, QKV_WIDTH:]
        ext_s = jnp.concatenate(
            [jnp.zeros((bs, POOL_HALO - POOL_PREFIX, POOL_WIDTH), _F32), state_pool[l], u_s],
            axis=1)
        p_s = _pool(ext_s, w_bd, b_pl, sc_pl, PAST_LEN)
        h_s = _mix(a_s.astype(_BF16), p_s, xs, w_out_bf, g1, b1, alpha, bs * ts)
        xs = _moe(h_s, w_router, b_router, w_gate, w_up, w_down, l, g2, b2, alpha, bs * ts)
        sk.append(proj_s3[:, :, ATTN_WIDTH:2 * ATTN_WIDTH].reshape(bs, ts, N_HEADS, HEAD_DIM))
        sv.append(proj_s3[:, :, 2 * ATTN_WIDTH:QKV_WIDTH].reshape(bs, ts, N_HEADS, HEAD_DIM))
        su.append(u_s)

    return (xp.reshape(bp, tp, d), xs.reshape(bs, ts, d), jnp.stack(pk), jnp.stack(pv),
            jnp.stack(pu), jnp.stack(sk), jnp.stack(sv), jnp.stack(su))
```

```python
import functools
import math

import jax
import jax.numpy as jnp
from jax import lax
from jax.experimental import pallas as pl
from jax.experimental.pallas import tpu as pltpu

N_HEADS = 12
HEAD_DIM = 64
ATTN_WIDTH = N_HEADS * HEAD_DIM
POOL_WIDTH = 256
PROJ_WIDTH = 3 * ATTN_WIDTH + POOL_WIDTH
QKV_WIDTH = 3 * ATTN_WIDTH
POOL_WINDOWS = (2, 4, 8, 16)
POOL_GROUP = POOL_WIDTH // len(POOL_WINDOWS)
POOL_PREFIX = max(POOL_WINDOWS) - 1
POOL_HALO = 16
DILATIONS = (1, 4, 16)
WINDOW_STEPS = 128
MAX_WINDOW = 2048
PAST_LEN = 8192
N_EXPERT_GROUPS = 4
EXPERTS_PER_GROUP = 4
N_EXPERTS = 16
D_EXPERT = 256
LN_EPS = 1e-5
NEG_INF = -1e30
QK_SCALE = HEAD_DIM ** -0.5
LOG2_E = math.log2(math.e)

LANES = 128
SUBLANES = 8
HEADS_PER_TILE = LANES // HEAD_DIM
N_HEAD_TILES = ATTN_WIDTH // LANES
MXU_WIDTH = 256

ATTN_BLOCK = WINDOW_STEPS
SUPER_BLOCK = ATTN_BLOCK * max(DILATIONS)
BLOCKS_PER_SUPER = SUPER_BLOCK // ATTN_BLOCK

_F32 = jnp.float32
_BF16 = jnp.bfloat16
_NT = (((1,), (1,)), ((), ()))


def _params(n_grid_dims, vmem_mib):
    return pltpu.CompilerParams(
        dimension_semantics=("arbitrary",) * n_grid_dims,
        vmem_limit_bytes=vmem_mib * 1024 * 1024)


def _layer_norm(z, g, b):
    mu = jnp.mean(z, axis=-1, keepdims=True)
    zc = z - mu
    var = jnp.mean(zc * zc, axis=-1, keepdims=True)
    return zc * lax.rsqrt(var + LN_EPS) * g + b


def _proj_sample_kernel(x_ref, w_ref, of_ref):
    xb = x_ref[...].astype(_BF16)
    for j in range(0, PROJ_WIDTH, MXU_WIDTH):
        of_ref[:, j:j + MXU_WIDTH] = jnp.dot(xb, w_ref[:, j:j + MXU_WIDTH],
                                             preferred_element_type=_F32)


def _proj_sample(x, w_in_bf):
    n, d = x.shape
    return pl.pallas_call(
        _proj_sample_kernel,
        grid=(1,),
        in_specs=[pl.BlockSpec((n, d), lambda i: (0, 0)),
                  pl.BlockSpec((d, PROJ_WIDTH), lambda i: (0, 0))],
        out_specs=pl.BlockSpec((n, PROJ_WIDTH), lambda i: (0, 0)),
        out_shape=jax.ShapeDtypeStruct((n, PROJ_WIDTH), _F32),
        compiler_params=_params(1, 48),
        name="proj_sample",
    )(x, w_in_bf)


def _proj_prompt_kernel(x_ref, w_ref, kv_ref, u_ref, ob_ref, o4_ref, o16_ref,
                        slab_scr, cls_scr):
    tm = x_ref.shape[0]
    xb = x_ref[...].astype(_BF16)
    n_slab = 0
    for j in range(0, PROJ_WIDTH, MXU_WIDTH):
        r = jnp.dot(xb, w_ref[:, j:j + MXU_WIDTH], preferred_element_type=_F32)
        if j >= QKV_WIDTH:
            u_ref[...] = r
            continue
        if j >= ATTN_WIDTH:
            kv_ref[:, j - ATTN_WIDTH:j - ATTN_WIDTH + MXU_WIDTH] = r
        if j < ATTN_WIDTH:
            r = r * (QK_SCALE * LOG2_E)
        ob_ref[:, j:j + MXU_WIDTH] = r.astype(_BF16)
        for half in range(MXU_WIDTH // LANES):
            lanes = slice(j + half * LANES, j + (half + 1) * LANES)
            buf = n_slab % 2
            n_slab += 1
            slab_scr[buf] = r[:, half * LANES:(half + 1) * LANES]
            for c4 in range(4):
                x4 = slab_scr[buf, pl.ds(c4, tm // 4, stride=4), :]
                o4_ref[c4, :, lanes] = x4.astype(_BF16)
                cls_scr[buf, c4] = x4
                for c in range(4):
                    x16 = cls_scr[buf, c4, pl.ds(c, tm // 16, stride=4), :]
                    o16_ref[4 * c + c4, :, lanes] = x16.astype(_BF16)


def _proj_prompt(x, w_in_bf, batch, seq, win, tm):
    n, d = x.shape
    per_batch = seq // tm
    win_tiles = win // tm
    first_win = per_batch - win_tiles

    def cls(r):
        return pl.BlockSpec((r, tm // r, QKV_WIDTH),
                            lambda i: (i // per_batch, i % per_batch, 0))

    def kv_block(i):
        return ((i // per_batch) * win_tiles
                + jnp.maximum(i % per_batch - first_win, 0), 0)
    return pl.pallas_call(
        _proj_prompt_kernel,
        grid=(n // tm,),
        in_specs=[pl.BlockSpec((tm, d), lambda i: (i, 0)),
                  pl.BlockSpec((d, PROJ_WIDTH), lambda i: (0, 0))],
        out_specs=[pl.BlockSpec((tm, 2 * ATTN_WIDTH), kv_block),
                   pl.BlockSpec((tm, POOL_WIDTH), lambda i: (i, 0)),
                   pl.BlockSpec((tm, QKV_WIDTH), lambda i: (i, 0)),
                   cls(4), cls(16)],
        out_shape=[jax.ShapeDtypeStruct((batch * win, 2 * ATTN_WIDTH), _F32),
                   jax.ShapeDtypeStruct((n, POOL_WIDTH), _F32),
                   jax.ShapeDtypeStruct((n, QKV_WIDTH), _BF16),
                   jax.ShapeDtypeStruct((batch * 4, seq // 4, QKV_WIDTH), _BF16),
                   jax.ShapeDtypeStruct((batch * 16, seq // 16, QKV_WIDTH), _BF16)],
        scratch_shapes=[pltpu.VMEM((2, tm, LANES), _F32),
                        pltpu.VMEM((2, 4, tm // 4, LANES), _F32)],
        compiler_params=_params(1, 48),
        name="proj_prompt",
    )(x, w_in_bf)


def _attn_prompt_kernel(q1, q4, q16, k1, k1h, k4, k4h, k16, k16h,
                        v1, v1h, v4, v4h, v16, v16h, out_ref,
                        o_scr, m_scr, l_scr, bias_scr):
    blk = ATTN_BLOCK
    first_super = pl.program_id(2) == 0
    ii = lax.broadcasted_iota(jnp.int32, (blk, 2 * blk), 0)
    jj = lax.broadcasted_iota(jnp.int32, (blk, 2 * blk), 1)
    band = (jj >= ii) & (jj <= ii + WINDOW_STEPS)
    lo = jnp.where(first_super, blk, 0)
    bias_scr[0] = jnp.where(band, 0.0, NEG_INF)
    bias_scr[1] = jnp.where(band & (jj >= lo), 0.0, NEG_INF)
    head0 = lax.broadcasted_iota(jnp.int32, (blk, LANES), 1) < HEAD_DIM

    def block(g, r, q_ref, k_ref, kh_ref, v_ref, vh_ref, c, ib):
        off = ib * blk
        q = q_ref[c, off:off + blk, :]
        if ib == 0:
            bias = bias_scr[1]
            k2 = jnp.concatenate([kh_ref[c], k_ref[c, 0:blk, :]], axis=0)
            v2 = jnp.concatenate([vh_ref[c], v_ref[c, 0:blk, :]], axis=0)
        else:
            bias = bias_scr[0]
            k2 = k_ref[c, off - blk:off + blk, :]
            v2 = v_ref[c, off - blk:off + blk, :]
        zero = jnp.zeros_like(q)
        qq = jnp.concatenate([jnp.where(head0, q, zero), jnp.where(head0, zero, q)], axis=0)
        s = lax.dot_general(qq, k2, _NT, preferred_element_type=_F32)
        ms, ls, ps = [], [], []
        for h in range(HEADS_PER_TILE):
            sh = s[h * blk:(h + 1) * blk, :] + bias
            m = jnp.max(sh, axis=-1, keepdims=True)
            p = jnp.exp2(sh - m)
            ms.append(m)
            ls.append(jnp.sum(p, axis=-1, keepdims=True))
            ps.append(p.astype(_BF16))
        o = jnp.dot(jnp.concatenate(ps, axis=0), v2, preferred_element_type=_F32)
        if r == 1:
            rows = pl.ds(off, blk)
        else:
            rows = pl.ds(c + r * off, blk, stride=r)
        m_scr[g, rows, :] = jnp.where(head0, ms[0], ms[1])
        l_scr[g, rows, :] = jnp.where(head0, ls[0], ls[1])
        o_scr[g, rows, :] = jnp.where(head0, o[0:blk, :], o[blk:2 * blk, :])

    for g, (r, q_ref, k_ref, kh_ref, v_ref, vh_ref) in enumerate((
            (1, q1, k1, k1h, v1, v1h), (4, q4, k4, k4h, v4, v4h),
            (16, q16, k16, k16h, v16, v16h))):
        per_class = BLOCKS_PER_SUPER // r
        for c in range(r):
            for ib in range(per_class):
                block(g, r, q_ref, k_ref, kh_ref, v_ref, vh_ref, c, ib)

    chunk = 256

    def merge(ci, carry):
        rows = pl.ds(pl.multiple_of(ci * chunk, chunk), chunk)
        ms = [m_scr[g, rows, :] for g in range(3)]
        m = jnp.maximum(jnp.maximum(ms[0], ms[1]), ms[2])
        num = jnp.zeros((chunk, LANES), _F32)
        den = jnp.zeros((chunk, LANES), _F32)
        for g in range(3):
            a = jnp.exp2(ms[g] - m)
            num = num + a * o_scr[g, rows, :]
            den = den + a * l_scr[g, rows, :]
        out_ref[rows, :] = (num / den).astype(out_ref.dtype)
        return carry
    lax.fori_loop(0, SUPER_BLOCK // chunk, merge, 0)


def _attn_prompt(qkv, qkv4, qkv16, batch, seq):
    blk, sb = ATTN_BLOCK, SUPER_BLOCK
    n_super = seq // sb
    kcol, vcol = N_HEAD_TILES, 2 * N_HEAD_TILES
    views = {1: qkv.reshape(batch, seq, QKV_WIDTH), 4: qkv4, 16: qkv16}

    def main(r, col):
        return pl.BlockSpec((r, sb // r, LANES), lambda b, hp, s: (b, s, col + hp))

    def halo(r, col):
        per = sb // (r * blk)
        return pl.BlockSpec((r, blk, LANES),
                            lambda b, hp, s: (b, jnp.maximum(s * per - 1, 0), col + hp))

    in_specs = [main(r, 0) for r in DILATIONS]
    operands = [views[r] for r in DILATIONS]
    for col in (kcol, vcol):
        for r in DILATIONS:
            in_specs += [main(r, col), halo(r, col)]
            operands += [views[r], views[r]]
    return pl.pallas_call(
        _attn_prompt_kernel,
        grid=(batch, N_HEAD_TILES, n_super),
        in_specs=in_specs,
        out_specs=pl.BlockSpec((sb, LANES), lambda b, hp, s: (b * n_super + s, hp)),
        out_shape=jax.ShapeDtypeStruct((batch * seq, ATTN_WIDTH), _BF16),
        scratch_shapes=[
            pltpu.VMEM((3, sb, LANES), _F32),
            pltpu.VMEM((3, sb, LANES), _F32),
            pltpu.VMEM((3, sb, LANES), _F32),
            pltpu.VMEM((2, blk, 2 * blk), _F32)],
        compiler_params=_params(3, 48),
        name="attn_prompt",
    )(*operands)


def _branch_count(dist):
    return ((dist <= 128).astype(_F32)
            + (((dist & 3) == 0) & (dist <= 512)).astype(_F32)
            + (((dist & 15) == 0) & (dist <= MAX_WINDOW)).astype(_F32))


def _attn_sample_kernel(proj_ref, kt_ref, vt_ref, out_ref, *, n_new):
    rows = HEADS_PER_TILE * n_new
    t_c = lax.broadcasted_iota(jnp.int32, (rows, MAX_WINDOW), 0) & (n_new - 1)
    f_c = lax.broadcasted_iota(jnp.int32, (rows, MAX_WINDOW), 1)
    w_c = _branch_count(MAX_WINDOW + t_c - f_c)
    t_n = lax.broadcasted_iota(jnp.int32, (rows, LANES), 0) & (n_new - 1)
    f_n = lax.broadcasted_iota(jnp.int32, (rows, LANES), 1)
    d_n = t_n - f_n
    w_n = jnp.where((d_n >= 0) & (f_n < n_new), _branch_count(d_n), 0.0)
    head0 = lax.broadcasted_iota(jnp.int32, (n_new, LANES), 1) < HEAD_DIM
    pad = jnp.zeros((LANES - n_new, LANES), _F32)

    for hp in range(N_HEAD_TILES):
        sl = slice(hp * LANES, (hp + 1) * LANES)
        ksl = slice(ATTN_WIDTH + hp * LANES, ATTN_WIDTH + (hp + 1) * LANES)
        vsl = slice(2 * ATTN_WIDTH + hp * LANES, 2 * ATTN_WIDTH + (hp + 1) * LANES)
        heads = slice(hp * HEADS_PER_TILE, (hp + 1) * HEADS_PER_TILE)
        q = proj_ref[:, sl] * QK_SCALE
        q2 = jnp.concatenate([jnp.where(head0, q, 0.0),
                              jnp.where(head0, 0.0, q)], axis=0).astype(_BF16)
        kt = kt_ref[heads].reshape(LANES, MAX_WINDOW).astype(_BF16)
        vt = vt_ref[heads].reshape(LANES, MAX_WINDOW).astype(_BF16)
        k_n = jnp.concatenate([proj_ref[:, ksl], pad], axis=0).astype(_BF16)
        v_n = jnp.concatenate([proj_ref[:, vsl], pad], axis=0).astype(_BF16)
        s_c = jnp.dot(q2, kt, preferred_element_type=_F32)
        s_n = lax.dot_general(q2, k_n, _NT, preferred_element_type=_F32)
        s_c = jnp.where(w_c > 0.0, s_c, NEG_INF)
        s_n = jnp.where(w_n > 0.0, s_n, NEG_INF)
        m = jnp.maximum(jnp.max(s_c, axis=-1, keepdims=True),
                        jnp.max(s_n, axis=-1, keepdims=True))
        p_c = w_c * jnp.exp(s_c - m)
        p_n = w_n * jnp.exp(s_n - m)
        l = jnp.sum(p_c, axis=-1, keepdims=True) + jnp.sum(p_n, axis=-1, keepdims=True)
        o = (lax.dot_general(p_c.astype(_BF16), vt, _NT, preferred_element_type=_F32)
             + jnp.dot(p_n.astype(_BF16), v_n, preferred_element_type=_F32)) / l
        out_ref[:, sl] = jnp.where(head0, o[0:n_new, :], o[n_new:rows, :])


def _pool_rows(ext_ref, base, pos_first, w_ref, b_ref, sc_ref, bb, chunk):
    n = bb * chunk
    lane = lax.broadcasted_iota(jnp.int32, (n, POOL_WIDTH), 1)
    window = jnp.where(lane < POOL_GROUP, 2,
                       jnp.where(lane < 2 * POOL_GROUP, 4,
                                 jnp.where(lane < 3 * POOL_GROUP, 8, 16)))
    row = lax.broadcasted_iota(jnp.int32, (bb, chunk, POOL_WIDTH), 1).reshape(n, POOL_WIDTH)

    def shifted(i):
        lo = base + POOL_HALO - i
        return ext_ref[:, lo:lo + chunk, :].reshape(n, POOL_WIDTH)
    cur = shifted(0)
    acc = cur + shifted(1)
    total = acc
    for g, w in enumerate(POOL_WINDOWS[1:], start=1):
        for i in range(w // 2, w):
            acc = acc + shifted(i)
        total = jnp.where(lane >= g * POOL_GROUP, acc, total)
    count = jnp.minimum(pos_first + row + 1, window).astype(_F32)
    diff = total / count - cur
    y = jnp.dot(diff.astype(_BF16), w_ref[...], preferred_element_type=_F32)
    return (y + b_ref[...]) * sc_ref[...]


def _pool_kernel(ext_ref, w_ref, b_ref, sc_ref, out_ref, *, bb, chunk, pos0):
    y = _pool_rows(ext_ref, 0, pos0, w_ref, b_ref, sc_ref, bb, chunk)
    out_ref[...] = y.astype(out_ref.dtype)


def _pool(ext, w_bd, bias, scale, pos0):
    batch, rows, _ = ext.shape
    seq = rows - POOL_HALO
    return pl.pallas_call(
        functools.partial(_pool_kernel, bb=batch, chunk=seq, pos0=pos0),
        grid=(1,),
        in_specs=[pl.BlockSpec((batch, rows, POOL_WIDTH), lambda i: (0, 0, 0)),
                  pl.BlockSpec((POOL_WIDTH, POOL_WIDTH), lambda i: (0, 0)),
                  pl.BlockSpec((1, POOL_WIDTH), lambda i: (0, 0)),
                  pl.BlockSpec((1, POOL_WIDTH), lambda i: (0, 0))],
        out_specs=pl.BlockSpec((batch * seq, POOL_WIDTH), lambda i: (0, 0)),
        out_shape=jax.ShapeDtypeStruct((batch * seq, POOL_WIDTH), _BF16),
        compiler_params=_params(1, 48),
        name="pool",
    )(ext, w_bd, bias, scale)


def _mix_kernel(a_ref, p_ref, x_ref, w_ref, g_ref, b_ref, out_ref, *, alpha):
    mix = (jnp.dot(a_ref[...], w_ref[0:ATTN_WIDTH, :], preferred_element_type=_F32)
           + jnp.dot(p_ref[...], w_ref[ATTN_WIDTH:, :], preferred_element_type=_F32))
    out_ref[...] = _layer_norm(alpha * x_ref[...] + mix, g_ref[...], b_ref[...])


def _mix(a, p, x, w_out_bf, g, b, alpha, tm):
    n, d = x.shape
    return pl.pallas_call(
        functools.partial(_mix_kernel, alpha=alpha),
        grid=(n // tm,),
        in_specs=[pl.BlockSpec((tm, ATTN_WIDTH), lambda i: (i, 0)),
                  pl.BlockSpec((tm, POOL_WIDTH), lambda i: (i, 0)),
                  pl.BlockSpec((tm, d), lambda i: (i, 0)),
                  pl.BlockSpec((ATTN_WIDTH + POOL_WIDTH, d), lambda i: (0, 0)),
                  pl.BlockSpec((1, d), lambda i: (0, 0)),
                  pl.BlockSpec((1, d), lambda i: (0, 0))],
        out_specs=pl.BlockSpec((tm, d), lambda i: (i, 0)),
        out_shape=jax.ShapeDtypeStruct((n, d), _F32),
        compiler_params=_params(1, 48),
        name="mix",
    )(a, p, x, w_out_bf, g, b)


_POOL_CHUNK = 128


def _mix_prompt_kernel(a_ref, u_ref, uh_ref, x_ref, w_ref, wp_ref, bp_ref, sp_ref,
                       g_ref, b_ref, out_ref, ext_scr, p_scr, *, alpha, tiles_per_seq):
    tm = x_ref.shape[0]
    ti = pl.program_id(0) % tiles_per_seq
    ext_scr[0, 0:POOL_HALO, :] = jnp.where(ti == 0, 0.0, uh_ref[...])
    ext_scr[0, POOL_HALO:, :] = u_ref[...]
    for r0 in range(0, tm, _POOL_CHUNK):
        y = _pool_rows(ext_scr, r0, ti * tm + r0, wp_ref, bp_ref, sp_ref, 1, _POOL_CHUNK)
        p_scr[r0:r0 + _POOL_CHUNK, :] = y.astype(_BF16)
    mix = (jnp.dot(a_ref[...], w_ref[0:ATTN_WIDTH, :], preferred_element_type=_F32)
           + jnp.dot(p_scr[...], w_ref[ATTN_WIDTH:, :], preferred_element_type=_F32))
    out_ref[...] = _layer_norm(alpha * x_ref[...] + mix, g_ref[...], b_ref[...])


def _mix_prompt_attn_sample_kernel(a_ref, u_ref, uh_ref, x_ref, w_ref, wp_ref, bp_ref, sp_ref,
                                   g_ref, b_ref, proj_ref, kt_ref, vt_ref, h_ref, attn_ref,
                                   ext_scr, p_scr, *, alpha, tiles_per_seq, n_new):
    _mix_prompt_kernel(a_ref, u_ref, uh_ref, x_ref, w_ref, wp_ref, bp_ref, sp_ref,
                       g_ref, b_ref, h_ref, ext_scr, p_scr,
                       alpha=alpha, tiles_per_seq=tiles_per_seq)
    _attn_sample_kernel(proj_ref, kt_ref, vt_ref, attn_ref, n_new=n_new)


def _mix_prompt_attn_sample(a, u, x, w_out_bf, w_bd, b_pool, s_pool, g, b, alpha, seq,
                            proj_s, cache_kt, cache_vt, layer, n_new):
    n, d = x.shape
    batch_s = proj_s.shape[0] // n_new
    tm = n // batch_s
    assert n == tm * batch_s and seq % tm == 0 and tm % _POOL_CHUNK == 0
    halo_per_tile = tm // POOL_HALO
    cache_spec = pl.BlockSpec((None, None, N_HEADS, HEAD_DIM, MAX_WINDOW),
                              lambda i: (layer, i, 0, 0, 0))
    return pl.pallas_call(
        functools.partial(_mix_prompt_attn_sample_kernel, alpha=alpha,
                          tiles_per_seq=seq // tm, n_new=n_new),
        grid=(batch_s,),
        in_specs=[pl.BlockSpec((tm, ATTN_WIDTH), lambda i: (i, 0)),
                  pl.BlockSpec((tm, POOL_WIDTH), lambda i: (i, 0)),
                  pl.BlockSpec((POOL_HALO, POOL_WIDTH),
                               lambda i: (jnp.maximum(i * halo_per_tile - 1, 0), 0)),
                  pl.BlockSpec((tm, d), lambda i: (i, 0)),
                  pl.BlockSpec((ATTN_WIDTH + POOL_WIDTH, d), lambda i: (0, 0)),
                  pl.BlockSpec((POOL_WIDTH, POOL_WIDTH), lambda i: (0, 0)),
                  pl.BlockSpec((1, POOL_WIDTH), lambda i: (0, 0)),
                  pl.BlockSpec((1, POOL_WIDTH), lambda i: (0, 0)),
                  pl.BlockSpec((1, d), lambda i: (0, 0)),
                  pl.BlockSpec((1, d), lambda i: (0, 0)),
                  pl.BlockSpec((n_new, PROJ_WIDTH), lambda i: (i, 0)),
                  cache_spec, cache_spec],
        out_specs=[pl.BlockSpec((tm, d), lambda i: (i, 0)),
                   pl.BlockSpec((n_new, ATTN_WIDTH), lambda i: (i, 0))],
        out_shape=[jax.ShapeDtypeStruct((n, d), _F32),
                   jax.ShapeDtypeStruct((batch_s * n_new, ATTN_WIDTH), _F32)],
        scratch_shapes=[pltpu.VMEM((1, tm + POOL_HALO, POOL_WIDTH), _F32),
                        pltpu.VMEM((tm, POOL_WIDTH), _BF16)],
        compiler_params=_params(1, 56),
        name="mix_prompt_attn_sample",
    )(a, u, u, x, w_out_bf, w_bd, b_pool, s_pool, g, b, proj_s, cache_kt, cache_vt)


_ROUTER_LANES = LANES
_EXPERT_LANE0 = N_EXPERT_GROUPS
_MOE_ROW_CHUNK = 256


def _router_combine(h, hb, wr_ref, br_ref):
    h_lo = (h - hb.astype(_F32)).astype(_BF16)
    hi_pass = jnp.dot(hb, wr_ref[...], preferred_element_type=_F32)
    lo_pass = jnp.dot(h_lo, wr_ref[:, 0:_ROUTER_LANES], preferred_element_type=_F32)
    logits = (hi_pass[:, 0:_ROUTER_LANES] + hi_pass[:, _ROUTER_LANES:]
              + lo_pass + br_ref[...])
    lane = lax.broadcasted_iota(jnp.int32, logits.shape, 1)
    big = _ROUTER_LANES
    g_logit = jnp.where(lane < N_EXPERT_GROUPS, logits, NEG_INF)
    g_max = jnp.max(g_logit, axis=-1, keepdims=True)
    g_sel = jnp.min(jnp.where(g_logit == g_max, lane, big), axis=-1, keepdims=True)
    p_group = 1.0 / jnp.sum(jnp.exp(g_logit - g_max), axis=-1, keepdims=True)
    lane_group = (lane - _EXPERT_LANE0) >> 2
    in_group = ((lane >= _EXPERT_LANE0) & (lane < _EXPERT_LANE0 + N_EXPERTS)
                & (lane_group == g_sel))
    e_logit = jnp.where(in_group, logits, NEG_INF)
    v1 = jnp.max(e_logit, axis=-1, keepdims=True)
    i1 = jnp.min(jnp.where(e_logit == v1, lane, big), axis=-1, keepdims=True)
    e_rest = jnp.where(lane == i1, NEG_INF, e_logit)
    v2 = jnp.max(e_rest, axis=-1, keepdims=True)
    i2 = jnp.min(jnp.where(e_rest == v2, lane, big), axis=-1, keepdims=True)
    t = jnp.exp(v2 - v1)
    w1 = 1.0 / (1.0 + t)
    w2 = t / (1.0 + t)
    return p_group * (jnp.where(lane == i1, w1, 0.0) + jnp.where(lane == i2, w2, 0.0))


def _moe_kernel(h_ref, wr_ref, br_ref, wg_ref, wu_ref, wd_ref, g_ref, b_ref, out_ref,
                hb_scr, comb_scr, *, alpha):
    e = pl.program_id(1)
    tm = h_ref.shape[0]
    chunk = min(tm, _MOE_ROW_CHUNK)

    @pl.when(e == 0)
    def _():
        for r0 in range(0, tm, chunk):
            h = h_ref[r0:r0 + chunk, :]
            hb = h.astype(_BF16)
            hb_scr[r0:r0 + chunk, :] = hb
            comb_scr[r0:r0 + chunk, :] = _router_combine(h, hb, wr_ref, br_ref)
        out_ref[...] = jnp.zeros_like(out_ref)

    comb = comb_scr[...]
    lane = lax.broadcasted_iota(jnp.int32, comb.shape, 1)
    gate = jnp.sum(jnp.where(lane == _EXPERT_LANE0 + e, comb, 0.0), axis=-1, keepdims=True)
    hb = hb_scr[...]
    hg = jnp.dot(hb, wg_ref[...].astype(_BF16), preferred_element_type=_F32)
    hu = jnp.dot(hb, wu_ref[...].astype(_BF16), preferred_element_type=_F32)
    act = hg * (1.0 / (1.0 + jnp.exp(-hg))) * hu * gate
    out_ref[...] += jnp.dot(act.astype(_BF16), wd_ref[...].astype(_BF16),
                            preferred_element_type=_F32)

    @pl.when(e == N_EXPERTS - 1)
    def _():
        out_ref[...] = _layer_norm(alpha * h_ref[...] + out_ref[...], g_ref[...], b_ref[...])


def _moe(h, w_router, b_router, w_gate, w_up, w_down, layer, g, b, alpha, tm):
    n, d = h.shape
    return pl.pallas_call(
        functools.partial(_moe_kernel, alpha=alpha),
        grid=(n // tm, N_EXPERTS),
        in_specs=[pl.BlockSpec((tm, d), lambda i, e: (i, 0)),
                  pl.BlockSpec((d, 2 * _ROUTER_LANES), lambda i, e: (0, 0)),
                  pl.BlockSpec((1, _ROUTER_LANES), lambda i, e: (0, 0)),
                  pl.BlockSpec((None, None, d, D_EXPERT), lambda i, e: (layer, e, 0, 0)),
                  pl.BlockSpec((None, None, d, D_EXPERT), lambda i, e: (layer, e, 0, 0)),
                  pl.BlockSpec((None, None, D_EXPERT, d), lambda i, e: (layer, e, 0, 0)),
                  pl.BlockSpec((1, d), lambda i, e: (0, 0)),
                  pl.BlockSpec((1, d), lambda i, e: (0, 0))],
        out_specs=pl.BlockSpec((tm, d), lambda i, e: (i, 0)),
        out_shape=jax.ShapeDtypeStruct((n, d), _F32),
        scratch_shapes=[pltpu.VMEM((tm, d), _BF16),
                        pltpu.VMEM((tm, _ROUTER_LANES), _F32)],
        compiler_params=_params(2, 60),
        name="moe",
    )(h, w_router, b_router, w_gate, w_up, w_down, g, b)


def _block_diag(w_pool):
    n = len(POOL_WINDOWS)
    eye = jnp.eye(n, dtype=w_pool.dtype)
    return jnp.einsum('gce,gh->gche', w_pool, eye).reshape(POOL_WIDTH, POOL_WIDTH)


def _router_weights(w_rg, b_rg, w_re, b_re):
    d = w_rg.shape[0]
    w_e = jnp.transpose(w_re, (1, 0, 2)).reshape(d, N_EXPERTS)
    w = jnp.concatenate([w_rg, w_e], axis=1)
    w = jnp.pad(w, ((0, 0), (0, _ROUTER_LANES - w.shape[1])))
    bias = jnp.concatenate([b_rg, b_re.reshape(N_EXPERTS)])
    bias = jnp.pad(bias, (0, _ROUTER_LANES - bias.shape[0])).reshape(1, _ROUTER_LANES)
    w_hi = w.astype(_BF16)
    w_lo = (w - w_hi.astype(_F32)).astype(_BF16)
    return jnp.concatenate([w_hi, w_lo], axis=1), bias


def _row_tile(n, target):
    return target if n % target == 0 else n


def kernel(x_prompt, x_sample, cache_k, cache_v, state_pool, w_in, w_out, w_pool, b_pool,
           pool_scale, ln1_g, ln1_b, w_rg, b_rg, w_re, b_re, w_gate, w_up, w_down, ln2_g, ln2_b):
    depth = w_in.shape[0]
    alpha = float((2 * depth) ** 0.25)
    bp, tp, d = x_prompt.shape
    bs, ts, _ = x_sample.shape
    kv_rows = cache_k.shape[2]
    assert kv_rows == MAX_WINDOW and tp % SUPER_BLOCK == 0 and ts == SUBLANES
    win_p = min(MAX_WINDOW, tp)

    xp = x_prompt.reshape(bp * tp, d)
    xs = x_sample.reshape(bs * ts, d)
    ckt = jnp.transpose(cache_k, (0, 1, 3, 4, 2))
    cvt = jnp.transpose(cache_v, (0, 1, 3, 4, 2))

    pk, pv, pu, sk, sv, su = [], [], [], [], [], []
    for l in range(depth):
        w_in_bf = w_in[l].astype(_BF16)
        w_out_bf = w_out[l].astype(_BF16)
        w_bd = _block_diag(w_pool[l]).astype(_BF16)
        b_pl = b_pool[l].reshape(1, POOL_WIDTH)
        sc_pl = pool_scale[l].reshape(1, POOL_WIDTH)
        w_router, b_router = _router_weights(w_rg[l], b_rg[l], w_re[l], b_re[l])
        g1, b1 = ln1_g[l].reshape(1, d), ln1_b[l].reshape(1, d)
        g2, b2 = ln2_g[l].reshape(1, d), ln2_b[l].reshape(1, d)

        kv_p, u_p, qkv_p, qkv4_p, qkv16_p = _proj_prompt(xp, w_in_bf, bp, tp, win_p, 512)
        a_p = _attn_prompt(qkv_p, qkv4_p, qkv16_p, bp, tp)
        proj_s = _proj_sample(xs, w_in_bf)
        h_p, a_s = _mix_prompt_attn_sample(a_p, u_p, xp, w_out_bf, w_bd, b_pl, sc_pl, g1, b1,
                                           alpha, tp, proj_s, ckt, cvt, l, ts)
        xp = _moe(h_p, w_router, b_router, w_gate, w_up, w_down, l, g2, b2, alpha,
                  _row_tile(bp * tp, 2048))
        kv_p3 = kv_p.reshape(bp, win_p, 2 * ATTN_WIDTH)
        pk.append(kv_p3[:, :, 0:ATTN_WIDTH].reshape(bp, win_p, N_HEADS, HEAD_DIM))
        pv.append(kv_p3[:, :, ATTN_WIDTH:].reshape(bp, win_p, N_HEADS, HEAD_DIM))
        pu.append(u_p.reshape(bp, tp, POOL_WIDTH)[:, tp - POOL_PREFIX:])

        proj_s3 = proj_s.reshape(bs, ts, PROJ_WIDTH)
        u_s = proj_s3[:, :, QKV_WIDTH:]
        ext_s = jnp.concatenate(
            [jnp.zeros((bs, POOL_HALO - POOL_PREFIX, POOL_WIDTH), _F32), state_pool[l], u_s],
            axis=1)
        p_s = _pool(ext_s, w_bd, b_pl, sc_pl, PAST_LEN)
        h_s = _mix(a_s.astype(_BF16), p_s, xs, w_out_bf, g1, b1, alpha, bs * ts)
        xs = _moe(h_s, w_router, b_router, w_gate, w_up, w_down, l, g2, b2, alpha, bs * ts)
        sk.append(proj_s3[:, :, ATTN_WIDTH:2 * ATTN_WIDTH].reshape(bs, ts, N_HEADS, HEAD_DIM))
        sv.append(proj_s3[:, :, 2 * ATTN_WIDTH:QKV_WIDTH].reshape(bs, ts, N_HEADS, HEAD_DIM))
        su.append(u_s)

    return (xp.reshape(bp, tp, d), xs.reshape(bs, ts, d), jnp.stack(pk), jnp.stack(pv),
            jnp.stack(pu), jnp.stack(sk), jnp.stack(sv), jnp.stack(su))
```

```python
import functools
import math

import jax
import jax.numpy as jnp
from jax import lax
from jax.experimental import pallas as pl
from jax.experimental.pallas import tpu as pltpu

N_HEADS = 12
HEAD_DIM = 64
ATTN_WIDTH = N_HEADS * HEAD_DIM
POOL_WIDTH = 256
PROJ_WIDTH = 3 * ATTN_WIDTH + POOL_WIDTH
QKV_WIDTH = 3 * ATTN_WIDTH
POOL_WINDOWS = (2, 4, 8, 16)
POOL_GROUP = POOL_WIDTH // len(POOL_WINDOWS)
POOL_PREFIX = max(POOL_WINDOWS) - 1
POOL_HALO = 16
DILATIONS = (1, 4, 16)
WINDOW_STEPS = 128
MAX_WINDOW = 2048
PAST_LEN = 8192
N_EXPERT_GROUPS = 4
EXPERTS_PER_GROUP = 4
N_EXPERTS = 16
D_EXPERT = 256
LN_EPS = 1e-5
NEG_INF = -1e30
QK_SCALE = HEAD_DIM ** -0.5
LOG2_E = math.log2(math.e)

LANES = 128
SUBLANES = 8
HEADS_PER_TILE = LANES // HEAD_DIM
N_HEAD_TILES = ATTN_WIDTH // LANES
MXU_WIDTH = 256

ATTN_BLOCK = WINDOW_STEPS
SUPER_BLOCK = ATTN_BLOCK * max(DILATIONS)
BLOCKS_PER_SUPER = SUPER_BLOCK // ATTN_BLOCK

_F32 = jnp.float32
_BF16 = jnp.bfloat16
_NT = (((1,), (1,)), ((), ()))


def _params(n_grid_dims, vmem_mib):
    return pltpu.CompilerParams(
        dimension_semantics=("arbitrary",) * n_grid_dims,
        vmem_limit_bytes=vmem_mib * 1024 * 1024)


def _layer_norm(z, g, b):
    mu = jnp.mean(z, axis=-1, keepdims=True)
    zc = z - mu
    var = jnp.mean(zc * zc, axis=-1, keepdims=True)
    return zc * lax.rsqrt(var + LN_EPS) * g + b


def _proj_sample_kernel(x_ref, w_ref, of_ref):
    xb = x_ref[...].astype(_BF16)
    for j in range(0, PROJ_WIDTH, MXU_WIDTH):
        of_ref[:, j:j + MXU_WIDTH] = jnp.dot(xb, w_ref[:, j:j + MXU_WIDTH],
                                             preferred_element_type=_F32)


def _proj_sample(x, w_in_bf):
    n, d = x.shape
    return pl.pallas_call(
        _proj_sample_kernel,
        grid=(1,),
        in_specs=[pl.BlockSpec((n, d), lambda i: (0, 0)),
                  pl.BlockSpec((d, PROJ_WIDTH), lambda i: (0, 0))],
        out_specs=pl.BlockSpec((n, PROJ_WIDTH), lambda i: (0, 0)),
        out_shape=jax.ShapeDtypeStruct((n, PROJ_WIDTH), _F32),
        compiler_params=_params(1, 48),
        name="proj_sample",
    )(x, w_in_bf)


def _proj_prompt_kernel(x_ref, w_ref, kv_ref, u_ref, ob_ref, o4_ref, o16_ref,
                        slab_scr, cls_scr):
    tm = x_ref.shape[0]
    xb = x_ref[...].astype(_BF16)
    n_slab = 0
    for j in range(0, PROJ_WIDTH, MXU_WIDTH):
        r = jnp.dot(xb, w_ref[:, j:j + MXU_WIDTH], preferred_element_type=_F32)
        if j >= QKV_WIDTH:
            u_ref[...] = r
            continue
        if j >= ATTN_WIDTH:
            kv_ref[:, j - ATTN_WIDTH:j - ATTN_WIDTH + MXU_WIDTH] = r
        if j < ATTN_WIDTH:
            r = r * (QK_SCALE * LOG2_E)
        ob_ref[:, j:j + MXU_WIDTH] = r.astype(_BF16)
        for half in range(MXU_WIDTH // LANES):
            lanes = slice(j + half * LANES, j + (half + 1) * LANES)
            buf = n_slab % 2
            n_slab += 1
            slab_scr[buf] = r[:, half * LANES:(half + 1) * LANES]
            for c4 in range(4):
                x4 = slab_scr[buf, pl.ds(c4, tm // 4, stride=4), :]
                o4_ref[c4, :, lanes] = x4.astype(_BF16)
                cls_scr[buf, c4] = x4
                for c in range(4):
                    x16 = cls_scr[buf, c4, pl.ds(c, tm // 16, stride=4), :]
                    o16_ref[4 * c + c4, :, lanes] = x16.astype(_BF16)


def _proj_prompt(x, w_in_bf, batch, seq, win, tm):
    n, d = x.shape
    per_batch = seq // tm
    win_tiles = win // tm
    first_win = per_batch - win_tiles

    def cls(r):
        return pl.BlockSpec((r, tm // r, QKV_WIDTH),
                            lambda i: (i // per_batch, i % per_batch, 0))

    def kv_block(i):
        return ((i // per_batch) * win_tiles
                + jnp.maximum(i % per_batch - first_win, 0), 0)
    return pl.pallas_call(
        _proj_prompt_kernel,
        grid=(n // tm,),
        in_specs=[pl.BlockSpec((tm, d), lambda i: (i, 0)),
                  pl.BlockSpec((d, PROJ_WIDTH), lambda i: (0, 0))],
        out_specs=[pl.BlockSpec((tm, 2 * ATTN_WIDTH), kv_block),
                   pl.BlockSpec((tm, POOL_WIDTH), lambda i: (i, 0)),
                   pl.BlockSpec((tm, QKV_WIDTH), lambda i: (i, 0)),
                   cls(4), cls(16)],
        out_shape=[jax.ShapeDtypeStruct((batch * win, 2 * ATTN_WIDTH), _F32),
                   jax.ShapeDtypeStruct((n, POOL_WIDTH), _F32),
                   jax.ShapeDtypeStruct((n, QKV_WIDTH), _BF16),
                   jax.ShapeDtypeStruct((batch * 4, seq // 4, QKV_WIDTH), _BF16),
                   jax.ShapeDtypeStruct((batch * 16, seq // 16, QKV_WIDTH), _BF16)],
        scratch_shapes=[pltpu.VMEM((2, tm, LANES), _F32),
                        pltpu.VMEM((2, 4, tm // 4, LANES), _F32)],
        compiler_params=_params(1, 48),
        name="proj_prompt",
    )(x, w_in_bf)


def _attn_prompt_kernel(q1, q4, q16, k1, k1h, k4, k4h, k16, k16h,
                        v1, v1h, v4, v4h, v16, v16h, proj_s_ref, kt_ref, vt_ref,
                        out_ref, out_s_ref, o_scr, m_scr, l_scr, bias_scr, *, n_new, batch_s):
    step = ((pl.program_id(0) * pl.num_programs(1) + pl.program_id(1)) * pl.num_programs(2)
            + pl.program_id(2))

    @pl.when(step < batch_s)
    def _():
        _attn_sample_kernel(proj_s_ref, kt_ref, vt_ref, out_s_ref, n_new=n_new)

    blk = ATTN_BLOCK
    first_super = pl.program_id(2) == 0
    ii = lax.broadcasted_iota(jnp.int32, (blk, 2 * blk), 0)
    jj = lax.broadcasted_iota(jnp.int32, (blk, 2 * blk), 1)
    band = (jj >= ii) & (jj <= ii + WINDOW_STEPS)
    lo = jnp.where(first_super, blk, 0)
    bias_scr[0] = jnp.where(band, 0.0, NEG_INF)
    bias_scr[1] = jnp.where(band & (jj >= lo), 0.0, NEG_INF)
    head0 = lax.broadcasted_iota(jnp.int32, (blk, LANES), 1) < HEAD_DIM

    def block(g, r, q_ref, k_ref, kh_ref, v_ref, vh_ref, c, ib):
        off = ib * blk
        q = q_ref[c, off:off + blk, :]
        if ib == 0:
            bias = bias_scr[1]
            k2 = jnp.concatenate([kh_ref[c], k_ref[c, 0:blk, :]], axis=0)
            v2 = jnp.concatenate([vh_ref[c], v_ref[c, 0:blk, :]], axis=0)
        else:
            bias = bias_scr[0]
            k2 = k_ref[c, off - blk:off + blk, :]
            v2 = v_ref[c, off - blk:off + blk, :]
        zero = jnp.zeros_like(q)
        qq = jnp.concatenate([jnp.where(head0, q, zero), jnp.where(head0, zero, q)], axis=0)
        s = lax.dot_general(qq, k2, _NT, preferred_element_type=_F32)
        ms, ls, ps = [], [], []
        for h in range(HEADS_PER_TILE):
            sh = s[h * blk:(h + 1) * blk, :] + bias
            m = jnp.max(sh, axis=-1, keepdims=True)
            p = jnp.exp2(sh - m)
            ms.append(m)
            ls.append(jnp.sum(p, axis=-1, keepdims=True))
            ps.append(p.astype(_BF16))
        o = jnp.dot(jnp.concatenate(ps, axis=0), v2, preferred_element_type=_F32)
        if r == 1:
            rows = pl.ds(off, blk)
        else:
            rows = pl.ds(c + r * off, blk, stride=r)
        m_scr[g, rows, :] = jnp.where(head0, ms[0], ms[1])
        l_scr[g, rows, :] = jnp.where(head0, ls[0], ls[1])
        o_scr[g, rows, :] = jnp.where(head0, o[0:blk, :], o[blk:2 * blk, :])

    for g, (r, q_ref, k_ref, kh_ref, v_ref, vh_ref) in enumerate((
            (1, q1, k1, k1h, v1, v1h), (4, q4, k4, k4h, v4, v4h),
            (16, q16, k16, k16h, v16, v16h))):
        per_class = BLOCKS_PER_SUPER // r
        for c in range(r):
            for ib in range(per_class):
                block(g, r, q_ref, k_ref, kh_ref, v_ref, vh_ref, c, ib)

    chunk = 256

    def merge(ci, carry):
        rows = pl.ds(pl.multiple_of(ci * chunk, chunk), chunk)
        ms = [m_scr[g, rows, :] for g in range(3)]
        m = jnp.maximum(jnp.maximum(ms[0], ms[1]), ms[2])
        num = jnp.zeros((chunk, LANES), _F32)
        den = jnp.zeros((chunk, LANES), _F32)
        for g in range(3):
            a = jnp.exp2(ms[g] - m)
            num = num + a * o_scr[g, rows, :]
            den = den + a * l_scr[g, rows, :]
        out_ref[rows, :] = (num / den).astype(out_ref.dtype)
        return carry
    lax.fori_loop(0, SUPER_BLOCK // chunk, merge, 0)


def _attn_prompt(qkv, qkv4, qkv16, batch, seq, proj_s, cache_kt, cache_vt, layer, n_new):
    blk, sb = ATTN_BLOCK, SUPER_BLOCK
    n_super = seq // sb
    batch_s = proj_s.shape[0] // n_new
    assert batch * N_HEAD_TILES * n_super >= batch_s

    def sample_idx(b, hp, s):
        return jnp.minimum((b * N_HEAD_TILES + hp) * n_super + s, batch_s - 1)
    cache_spec = pl.BlockSpec((None, None, N_HEADS, HEAD_DIM, MAX_WINDOW),
                              lambda b, hp, s: (layer, sample_idx(b, hp, s), 0, 0, 0))
    kcol, vcol = N_HEAD_TILES, 2 * N_HEAD_TILES
    views = {1: qkv.reshape(batch, seq, QKV_WIDTH), 4: qkv4, 16: qkv16}

    def main(r, col):
        return pl.BlockSpec((r, sb // r, LANES), lambda b, hp, s: (b, s, col + hp))

    def halo(r, col):
        per = sb // (r * blk)
        return pl.BlockSpec((r, blk, LANES),
                            lambda b, hp, s: (b, jnp.maximum(s * per - 1, 0), col + hp))

    in_specs = [main(r, 0) for r in DILATIONS]
    operands = [views[r] for r in DILATIONS]
    for col in (kcol, vcol):
        for r in DILATIONS:
            in_specs += [main(r, col), halo(r, col)]
            operands += [views[r], views[r]]
    in_specs += [pl.BlockSpec((n_new, PROJ_WIDTH),
                              lambda b, hp, s: (sample_idx(b, hp, s), 0)),
                 cache_spec, cache_spec]
    operands += [proj_s, cache_kt, cache_vt]
    return pl.pallas_call(
        functools.partial(_attn_prompt_kernel, n_new=n_new, batch_s=batch_s),
        grid=(batch, N_HEAD_TILES, n_super),
        in_specs=in_specs,
        out_specs=[pl.BlockSpec((sb, LANES), lambda b, hp, s: (b * n_super + s, hp)),
                   pl.BlockSpec((n_new, ATTN_WIDTH),
                                lambda b, hp, s: (sample_idx(b, hp, s), 0))],
        out_shape=[jax.ShapeDtypeStruct((batch * seq, ATTN_WIDTH), _BF16),
                   jax.ShapeDtypeStruct((batch_s * n_new, ATTN_WIDTH), _F32)],
        scratch_shapes=[
            pltpu.VMEM((3, sb, LANES), _F32),
            pltpu.VMEM((3, sb, LANES), _F32),
            pltpu.VMEM((3, sb, LANES), _F32),
            pltpu.VMEM((2, blk, 2 * blk), _F32)],
        compiler_params=_params(3, 58),
        name="attn_prompt",
    )(*operands)


def _branch_count(dist):
    return ((dist <= 128).astype(_F32)
            + (((dist & 3) == 0) & (dist <= 512)).astype(_F32)
            + (((dist & 15) == 0) & (dist <= MAX_WINDOW)).astype(_F32))


def _attn_sample_kernel(proj_ref, kt_ref, vt_ref, out_ref, *, n_new):
    rows = HEADS_PER_TILE * n_new
    t_c = lax.broadcasted_iota(jnp.int32, (rows, MAX_WINDOW), 0) & (n_new - 1)
    f_c = lax.broadcasted_iota(jnp.int32, (rows, MAX_WINDOW), 1)
    w_c = _branch_count(MAX_WINDOW + t_c - f_c)
    t_n = lax.broadcasted_iota(jnp.int32, (rows, LANES), 0) & (n_new - 1)
    f_n = lax.broadcasted_iota(jnp.int32, (rows, LANES), 1)
    d_n = t_n - f_n
    w_n = jnp.where((d_n >= 0) & (f_n < n_new), _branch_count(d_n), 0.0)
    head0 = lax.broadcasted_iota(jnp.int32, (n_new, LANES), 1) < HEAD_DIM
    pad = jnp.zeros((LANES - n_new, LANES), _F32)

    for hp in range(N_HEAD_TILES):
        sl = slice(hp * LANES, (hp + 1) * LANES)
        ksl = slice(ATTN_WIDTH + hp * LANES, ATTN_WIDTH + (hp + 1) * LANES)
        vsl = slice(2 * ATTN_WIDTH + hp * LANES, 2 * ATTN_WIDTH + (hp + 1) * LANES)
        heads = slice(hp * HEADS_PER_TILE, (hp + 1) * HEADS_PER_TILE)
        q = proj_ref[:, sl] * QK_SCALE
        q2 = jnp.concatenate([jnp.where(head0, q, 0.0),
                              jnp.where(head0, 0.0, q)], axis=0).astype(_BF16)
        kt = kt_ref[heads].reshape(LANES, MAX_WINDOW).astype(_BF16)
        vt = vt_ref[heads].reshape(LANES, MAX_WINDOW).astype(_BF16)
        k_n = jnp.concatenate([proj_ref[:, ksl], pad], axis=0).astype(_BF16)
        v_n = jnp.concatenate([proj_ref[:, vsl], pad], axis=0).astype(_BF16)
        s_c = jnp.dot(q2, kt, preferred_element_type=_F32)
        s_n = lax.dot_general(q2, k_n, _NT, preferred_element_type=_F32)
        s_c = jnp.where(w_c > 0.0, s_c, NEG_INF)
        s_n = jnp.where(w_n > 0.0, s_n, NEG_INF)
        m = jnp.maximum(jnp.max(s_c, axis=-1, keepdims=True),
                        jnp.max(s_n, axis=-1, keepdims=True))
        p_c = w_c * jnp.exp(s_c - m)
        p_n = w_n * jnp.exp(s_n - m)
        l = jnp.sum(p_c, axis=-1, keepdims=True) + jnp.sum(p_n, axis=-1, keepdims=True)
        o = (lax.dot_general(p_c.astype(_BF16), vt, _NT, preferred_element_type=_F32)
             + jnp.dot(p_n.astype(_BF16), v_n, preferred_element_type=_F32)) / l
        out_ref[:, sl] = jnp.where(head0, o[0:n_new, :], o[n_new:rows, :])


def _pool_rows(ext_ref, base, pos_first, w_ref, b_ref, sc_ref, bb, chunk):
    n = bb * chunk
    lane = lax.broadcasted_iota(jnp.int32, (n, POOL_WIDTH), 1)
    window = jnp.where(lane < POOL_GROUP, 2,
                       jnp.where(lane < 2 * POOL_GROUP, 4,
                                 jnp.where(lane < 3 * POOL_GROUP, 8, 16)))
    row = lax.broadcasted_iota(jnp.int32, (bb, chunk, POOL_WIDTH), 1).reshape(n, POOL_WIDTH)

    def shifted(i):
        lo = base + POOL_HALO - i
        return ext_ref[:, lo:lo + chunk, :].reshape(n, POOL_WIDTH)
    cur = shifted(0)
    acc = cur + shifted(1)
    total = acc
    for g, w in enumerate(POOL_WINDOWS[1:], start=1):
        for i in range(w // 2, w):
            acc = acc + shifted(i)
        total = jnp.where(lane >= g * POOL_GROUP, acc, total)
    count = jnp.minimum(pos_first + row + 1, window).astype(_F32)
    diff = total / count - cur
    y = jnp.dot(diff.astype(_BF16), w_ref[...], preferred_element_type=_F32)
    return (y + b_ref[...]) * sc_ref[...]


def _pool_kernel(ext_ref, w_ref, b_ref, sc_ref, out_ref, *, bb, chunk, pos0):
    y = _pool_rows(ext_ref, 0, pos0, w_ref, b_ref, sc_ref, bb, chunk)
    out_ref[...] = y.astype(out_ref.dtype)


def _pool(ext, w_bd, bias, scale, pos0):
    batch, rows, _ = ext.shape
    seq = rows - POOL_HALO
    return pl.pallas_call(
        functools.partial(_pool_kernel, bb=batch, chunk=seq, pos0=pos0),
        grid=(1,),
        in_specs=[pl.BlockSpec((batch, rows, POOL_WIDTH), lambda i: (0, 0, 0)),
                  pl.BlockSpec((POOL_WIDTH, POOL_WIDTH), lambda i: (0, 0)),
                  pl.BlockSpec((1, POOL_WIDTH), lambda i: (0, 0)),
                  pl.BlockSpec((1, POOL_WIDTH), lambda i: (0, 0))],
        out_specs=pl.BlockSpec((batch * seq, POOL_WIDTH), lambda i: (0, 0)),
        out_shape=jax.ShapeDtypeStruct((batch * seq, POOL_WIDTH), _BF16),
        compiler_params=_params(1, 48),
        name="pool",
    )(ext, w_bd, bias, scale)


def _mix_kernel(a_ref, p_ref, x_ref, w_ref, g_ref, b_ref, out_ref, *, alpha):
    mix = (jnp.dot(a_ref[...], w_ref[0:ATTN_WIDTH, :], preferred_element_type=_F32)
           + jnp.dot(p_ref[...], w_ref[ATTN_WIDTH:, :], preferred_element_type=_F32))
    out_ref[...] = _layer_norm(alpha * x_ref[...] + mix, g_ref[...], b_ref[...])


def _mix(a, p, x, w_out_bf, g, b, alpha, tm):
    n, d = x.shape
    return pl.pallas_call(
        functools.partial(_mix_kernel, alpha=alpha),
        grid=(n // tm,),
        in_specs=[pl.BlockSpec((tm, ATTN_WIDTH), lambda i: (i, 0)),
                  pl.BlockSpec((tm, POOL_WIDTH), lambda i: (i, 0)),
                  pl.BlockSpec((tm, d), lambda i: (i, 0)),
                  pl.BlockSpec((ATTN_WIDTH + POOL_WIDTH, d), lambda i: (0, 0)),
                  pl.BlockSpec((1, d), lambda i: (0, 0)),
                  pl.BlockSpec((1, d), lambda i: (0, 0))],
        out_specs=pl.BlockSpec((tm, d), lambda i: (i, 0)),
        out_shape=jax.ShapeDtypeStruct((n, d), _F32),
        compiler_params=_params(1, 48),
        name="mix",
    )(a, p, x, w_out_bf, g, b)


_POOL_CHUNK = 128


def _mix_prompt_kernel(a_ref, u_ref, uh_ref, x_ref, w_ref, wp_ref, bp_ref, sp_ref,
                       g_ref, b_ref, out_ref, ext_scr, p_scr, *, alpha, tiles_per_seq):
    tm = x_ref.shape[0]
    ti = pl.program_id(0) % tiles_per_seq
    ext_scr[0, 0:POOL_HALO, :] = jnp.where(ti == 0, 0.0, uh_ref[...])
    ext_scr[0, POOL_HALO:, :] = u_ref[...]
    for r0 in range(0, tm, _POOL_CHUNK):
        y = _pool_rows(ext_scr, r0, ti * tm + r0, wp_ref, bp_ref, sp_ref, 1, _POOL_CHUNK)
        p_scr[r0:r0 + _POOL_CHUNK, :] = y.astype(_BF16)
    mix = (jnp.dot(a_ref[...], w_ref[0:ATTN_WIDTH, :], preferred_element_type=_F32)
           + jnp.dot(p_scr[...], w_ref[ATTN_WIDTH:, :], preferred_element_type=_F32))
    out_ref[...] = _layer_norm(alpha * x_ref[...] + mix, g_ref[...], b_ref[...])


def _mix_prompt(a, u, x, w_out_bf, w_bd, b_pool, s_pool, g, b, alpha, seq, tm):
    n, d = x.shape
    assert seq % tm == 0 and tm % _POOL_CHUNK == 0
    halo_per_tile = tm // POOL_HALO
    return pl.pallas_call(
        functools.partial(_mix_prompt_kernel, alpha=alpha, tiles_per_seq=seq // tm),
        grid=(n // tm,),
        in_specs=[pl.BlockSpec((tm, ATTN_WIDTH), lambda i: (i, 0)),
                  pl.BlockSpec((tm, POOL_WIDTH), lambda i: (i, 0)),
                  pl.BlockSpec((POOL_HALO, POOL_WIDTH),
                               lambda i: (jnp.maximum(i * halo_per_tile - 1, 0), 0)),
                  pl.BlockSpec((tm, d), lambda i: (i, 0)),
                  pl.BlockSpec((ATTN_WIDTH + POOL_WIDTH, d), lambda i: (0, 0)),
                  pl.BlockSpec((POOL_WIDTH, POOL_WIDTH), lambda i: (0, 0)),
                  pl.BlockSpec((1, POOL_WIDTH), lambda i: (0, 0)),
                  pl.BlockSpec((1, POOL_WIDTH), lambda i: (0, 0)),
                  pl.BlockSpec((1, d), lambda i: (0, 0)),
                  pl.BlockSpec((1, d), lambda i: (0, 0))],
        out_specs=pl.BlockSpec((tm, d), lambda i: (i, 0)),
        out_shape=jax.ShapeDtypeStruct((n, d), _F32),
        scratch_shapes=[pltpu.VMEM((1, tm + POOL_HALO, POOL_WIDTH), _F32),
                        pltpu.VMEM((tm, POOL_WIDTH), _BF16)],
        compiler_params=_params(1, 48),
        name="mix_prompt",
    )(a, u, u, x, w_out_bf, w_bd, b_pool, s_pool, g, b)


_ROUTER_LANES = LANES
_EXPERT_LANE0 = N_EXPERT_GROUPS
_MOE_ROW_CHUNK = 256


def _router_combine(h, hb, wr_ref, br_ref):
    h_lo = (h - hb.astype(_F32)).astype(_BF16)
    hi_pass = jnp.dot(hb, wr_ref[...], preferred_element_type=_F32)
    lo_pass = jnp.dot(h_lo, wr_ref[:, 0:_ROUTER_LANES], preferred_element_type=_F32)
    logits = (hi_pass[:, 0:_ROUTER_LANES] + hi_pass[:, _ROUTER_LANES:]
              + lo_pass + br_ref[...])
    lane = lax.broadcasted_iota(jnp.int32, logits.shape, 1)
    big = _ROUTER_LANES
    g_logit = jnp.where(lane < N_EXPERT_GROUPS, logits, NEG_INF)
    g_max = jnp.max(g_logit, axis=-1, keepdims=True)
    g_sel = jnp.min(jnp.where(g_logit == g_max, lane, big), axis=-1, keepdims=True)
    p_group = 1.0 / jnp.sum(jnp.exp(g_logit - g_max), axis=-1, keepdims=True)
    lane_group = (lane - _EXPERT_LANE0) >> 2
    in_group = ((lane >= _EXPERT_LANE0) & (lane < _EXPERT_LANE0 + N_EXPERTS)
                & (lane_group == g_sel))
    e_logit = jnp.where(in_group, logits, NEG_INF)
    v1 = jnp.max(e_logit, axis=-1, keepdims=True)
    i1 = jnp.min(jnp.where(e_logit == v1, lane, big), axis=-1, keepdims=True)
    e_rest = jnp.where(lane == i1, NEG_INF, e_logit)
    v2 = jnp.max(e_rest, axis=-1, keepdims=True)
    i2 = jnp.min(jnp.where(e_rest == v2, lane, big), axis=-1, keepdims=True)
    t = jnp.exp(v2 - v1)
    w1 = 1.0 / (1.0 + t)
    w2 = t / (1.0 + t)
    return p_group * (jnp.where(lane == i1, w1, 0.0) + jnp.where(lane == i2, w2, 0.0))


def _moe_kernel(h_ref, wr_ref, br_ref, wg_ref, wu_ref, wd_ref, g_ref, b_ref, out_ref,
                hb_scr, comb_scr, *, alpha):
    e = pl.program_id(1)
    tm = h_ref.shape[0]
    chunk = min(tm, _MOE_ROW_CHUNK)

    @pl.when(e == 0)
    def _():
        for r0 in range(0, tm, chunk):
            h = h_ref[r0:r0 + chunk, :]
            hb = h.astype(_BF16)
            hb_scr[r0:r0 + chunk, :] = hb
            comb_scr[r0:r0 + chunk, :] = _router_combine(h, hb, wr_ref, br_ref)
        out_ref[...] = jnp.zeros_like(out_ref)

    comb = comb_scr[...]
    lane = lax.broadcasted_iota(jnp.int32, comb.shape, 1)
    gate = jnp.sum(jnp.where(lane == _EXPERT_LANE0 + e, comb, 0.0), axis=-1, keepdims=True)
    hb = hb_scr[...]
    hg = jnp.dot(hb, wg_ref[...].astype(_BF16), preferred_element_type=_F32)
    hu = jnp.dot(hb, wu_ref[...].astype(_BF16), preferred_element_type=_F32)
    act = hg * (1.0 / (1.0 + jnp.exp(-hg))) * hu * gate
    out_ref[...] += jnp.dot(act.astype(_BF16), wd_ref[...].astype(_BF16),
                            preferred_element_type=_F32)

    @pl.when(e == N_EXPERTS - 1)
    def _():
        out_ref[...] = _layer_norm(alpha * h_ref[...] + out_ref[...], g_ref[...], b_ref[...])


def _moe(h, w_router, b_router, w_gate, w_up, w_down, layer, g, b, alpha, tm):
    n, d = h.shape
    return pl.pallas_call(
        functools.partial(_moe_kernel, alpha=alpha),
        grid=(n // tm, N_EXPERTS),
        in_specs=[pl.BlockSpec((tm, d), lambda i, e: (i, 0)),
                  pl.BlockSpec((d, 2 * _ROUTER_LANES), lambda i, e: (0, 0)),
                  pl.BlockSpec((1, _ROUTER_LANES), lambda i, e: (0, 0)),
                  pl.BlockSpec((None, None, d, D_EXPERT), lambda i, e: (layer, e, 0, 0)),
                  pl.BlockSpec((None, None, d, D_EXPERT), lambda i, e: (layer, e, 0, 0)),
                  pl.BlockSpec((None, None, D_EXPERT, d), lambda i, e: (layer, e, 0, 0)),
                  pl.BlockSpec((1, d), lambda i, e: (0, 0)),
                  pl.BlockSpec((1, d), lambda i, e: (0, 0))],
        out_specs=pl.BlockSpec((tm, d), lambda i, e: (i, 0)),
        out_shape=jax.ShapeDtypeStruct((n, d), _F32),
        scratch_shapes=[pltpu.VMEM((tm, d), _BF16),
                        pltpu.VMEM((tm, _ROUTER_LANES), _F32)],
        compiler_params=_params(2, 60),
        name="moe",
    )(h, w_router, b_router, w_gate, w_up, w_down, g, b)


def _block_diag(w_pool):
    n = len(POOL_WINDOWS)
    eye = jnp.eye(n, dtype=w_pool.dtype)
    return jnp.einsum('gce,gh->gche', w_pool, eye).reshape(POOL_WIDTH, POOL_WIDTH)


def _router_weights(w_rg, b_rg, w_re, b_re):
    d = w_rg.shape[0]
    w_e = jnp.transpose(w_re, (1, 0, 2)).reshape(d, N_EXPERTS)
    w = jnp.concatenate([w_rg, w_e], axis=1)
    w = jnp.pad(w, ((0, 0), (0, _ROUTER_LANES - w.shape[1])))
    bias = jnp.concatenate([b_rg, b_re.reshape(N_EXPERTS)])
    bias = jnp.pad(bias, (0, _ROUTER_LANES - bias.shape[0])).reshape(1, _ROUTER_LANES)
    w_hi = w.astype(_BF16)
    w_lo = (w - w_hi.astype(_F32)).astype(_BF16)
    return jnp.concatenate([w_hi, w_lo], axis=1), bias


def _row_tile(n, target):
    return target if n % target == 0 else n


def kernel(x_prompt, x_sample, cache_k, cache_v, state_pool, w_in, w_out, w_pool, b_pool,
           pool_scale, ln1_g, ln1_b, w_rg, b_rg, w_re, b_re, w_gate, w_up, w_down, ln2_g, ln2_b):
    depth = w_in.shape[0]
    alpha = float((2 * depth) ** 0.25)
    bp, tp, d = x_prompt.shape
    bs, ts, _ = x_sample.shape
    kv_rows = cache_k.shape[2]
    assert kv_rows == MAX_WINDOW and tp % SUPER_BLOCK == 0 and ts == SUBLANES
    win_p = min(MAX_WINDOW, tp)

    xp = x_prompt.reshape(bp * tp, d)
    xs = x_sample.reshape(bs * ts, d)
    ckt = jnp.transpose(cache_k, (0, 1, 3, 4, 2))
    cvt = jnp.transpose(cache_v, (0, 1, 3, 4, 2))

    pk, pv, pu, sk, sv, su = [], [], [], [], [], []
    for l in range(depth):
        w_in_bf = w_in[l].astype(_BF16)
        w_out_bf = w_out[l].astype(_BF16)
        w_bd = _block_diag(w_pool[l]).astype(_BF16)
        b_pl = b_pool[l].reshape(1, POOL_WIDTH)
        sc_pl = pool_scale[l].reshape(1, POOL_WIDTH)
        w_router, b_router = _router_weights(w_rg[l], b_rg[l], w_re[l], b_re[l])
        g1, b1 = ln1_g[l].reshape(1, d), ln1_b[l].reshape(1, d)
        g2, b2 = ln2_g[l].reshape(1, d), ln2_b[l].reshape(1, d)

        kv_p, u_p, qkv_p, qkv4_p, qkv16_p = _proj_prompt(xp, w_in_bf, bp, tp, win_p, 512)
        proj_s = _proj_sample(xs, w_in_bf)
        a_p, a_s = _attn_prompt(qkv_p, qkv4_p, qkv16_p, bp, tp, proj_s, ckt, cvt, l, ts)
        h_p = _mix_prompt(a_p, u_p, xp, w_out_bf, w_bd, b_pl, sc_pl, g1, b1, alpha, tp, 512)
        xp = _moe(h_p, w_router, b_router, w_gate, w_up, w_down, l, g2, b2, alpha,
                  _row_tile(bp * tp, 2048))
        kv_p3 = kv_p.reshape(bp, win_p, 2 * ATTN_WIDTH)
        pk.append(kv_p3[:, :, 0:ATTN_WIDTH].reshape(bp, win_p, N_HEADS, HEAD_DIM))
        pv.append(kv_p3[:, :, ATTN_WIDTH:].reshape(bp, win_p, N_HEADS, HEAD_DIM))
        pu.append(u_p.reshape(bp, tp, POOL_WIDTH)[:, tp - POOL_PREFIX:])

        proj_s3 = proj_s.reshape(bs, ts, PROJ_WIDTH)
        u_s = proj_s3[:, :, QKV_WIDTH:]
        ext_s = jnp.concatenate(
            [jnp.zeros((bs, POOL_HALO - POOL_PREFIX, POOL_WIDTH), _F32), state_pool[l], u_s],
            axis=1)
        p_s = _pool(ext_s, w_bd, b_pl, sc_pl, PAST_LEN)
        h_s = _mix(a_s.astype(_BF16), p_s, xs, w_out_bf, g1, b1, alpha, bs * ts)
        xs = _moe(h_s, w_router, b_router, w_gate, w_up, w_down, l, g2, b2, alpha, bs * ts)
        sk.append(proj_s3[:, :, ATTN_WIDTH:2 * ATTN_WIDTH].reshape(bs, ts, N_HEADS, HEAD_DIM))
        sv.append(proj_s3[:, :, 2 * ATTN_WIDTH:QKV_WIDTH].reshape(bs, ts, N_HEADS, HEAD_DIM))
        su.append(u_s)

    return (xp.reshape(bp, tp, d), xs.reshape(bs, ts, d), jnp.stack(pk), jnp.stack(pv),
            jnp.stack(pu), jnp.stack(sk), jnp.stack(sv), jnp.stack(su))
```

```python
import functools
import math

import jax
import jax.numpy as jnp
from jax import lax
from jax.experimental import pallas as pl
from jax.experimental.pallas import tpu as pltpu

N_HEADS = 12
HEAD_DIM = 64
ATTN_WIDTH = N_HEADS * HEAD_DIM
POOL_WIDTH = 256
PROJ_WIDTH = 3 * ATTN_WIDTH + POOL_WIDTH
QKV_WIDTH = 3 * ATTN_WIDTH
POOL_WINDOWS = (2, 4, 8, 16)
POOL_GROUP = POOL_WIDTH // len(POOL_WINDOWS)
POOL_PREFIX = max(POOL_WINDOWS) - 1
POOL_HALO = 16
DILATIONS = (1, 4, 16)
WINDOW_STEPS = 128
MAX_WINDOW = 2048
PAST_LEN = 8192
N_EXPERT_GROUPS = 4
EXPERTS_PER_GROUP = 4
N_EXPERTS = 16
D_EXPERT = 256
LN_EPS = 1e-5
NEG_INF = -1e30
QK_SCALE = HEAD_DIM ** -0.5
LOG2_E = math.log2(math.e)

LANES = 128
SUBLANES = 8
HEADS_PER_TILE = LANES // HEAD_DIM
N_HEAD_TILES = ATTN_WIDTH // LANES
MXU_WIDTH = 256

ATTN_BLOCK = WINDOW_STEPS
SUPER_BLOCK = ATTN_BLOCK * max(DILATIONS)
BLOCKS_PER_SUPER = SUPER_BLOCK // ATTN_BLOCK

_F32 = jnp.float32
_BF16 = jnp.bfloat16
_NT = (((1,), (1,)), ((), ()))


def _params(n_grid_dims, vmem_mib):
    return pltpu.CompilerParams(
        dimension_semantics=("arbitrary",) * n_grid_dims,
        vmem_limit_bytes=vmem_mib * 1024 * 1024)


def _layer_norm(z, g, b):
    mu = jnp.mean(z, axis=-1, keepdims=True)
    zc = z - mu
    var = jnp.mean(zc * zc, axis=-1, keepdims=True)
    return zc * lax.rsqrt(var + LN_EPS) * g + b


def _proj_sample_kernel(x_ref, w_ref, of_ref):
    xb = x_ref[...].astype(_BF16)
    for j in range(0, PROJ_WIDTH, MXU_WIDTH):
        of_ref[:, j:j + MXU_WIDTH] = jnp.dot(xb, w_ref[:, j:j + MXU_WIDTH],
                                             preferred_element_type=_F32)


def _proj_sample(x, w_in_bf):
    n, d = x.shape
    return pl.pallas_call(
        _proj_sample_kernel,
        grid=(1,),
        in_specs=[pl.BlockSpec((n, d), lambda i: (0, 0)),
                  pl.BlockSpec((d, PROJ_WIDTH), lambda i: (0, 0))],
        out_specs=pl.BlockSpec((n, PROJ_WIDTH), lambda i: (0, 0)),
        out_shape=jax.ShapeDtypeStruct((n, PROJ_WIDTH), _F32),
        compiler_params=_params(1, 48),
        name="proj_sample",
    )(x, w_in_bf)


def _proj_prompt_kernel(x_ref, w_ref, kv_ref, u_ref, ob_ref, o4_ref, o16_ref,
                        slab_scr, cls_scr):
    tm = x_ref.shape[0]
    xb = x_ref[...].astype(_BF16)
    n_slab = 0
    for j in range(0, PROJ_WIDTH, MXU_WIDTH):
        r = jnp.dot(xb, w_ref[:, j:j + MXU_WIDTH], preferred_element_type=_F32)
        if j >= QKV_WIDTH:
            u_ref[...] = r
            continue
        if j >= ATTN_WIDTH:
            kv_ref[:, j - ATTN_WIDTH:j - ATTN_WIDTH + MXU_WIDTH] = r
        if j < ATTN_WIDTH:
            r = r * (QK_SCALE * LOG2_E)
        ob_ref[:, j:j + MXU_WIDTH] = r.astype(_BF16)
        for half in range(MXU_WIDTH // LANES):
            lanes = slice(j + half * LANES, j + (half + 1) * LANES)
            buf = n_slab % 2
            n_slab += 1
            slab_scr[buf] = r[:, half * LANES:(half + 1) * LANES]
            for c4 in range(4):
                x4 = slab_scr[buf, pl.ds(c4, tm // 4, stride=4), :]
                o4_ref[c4, :, lanes] = x4.astype(_BF16)
                cls_scr[buf, c4] = x4
                for c in range(4):
                    x16 = cls_scr[buf, c4, pl.ds(c, tm // 16, stride=4), :]
                    o16_ref[4 * c + c4, :, lanes] = x16.astype(_BF16)


def _proj_prompt(x, w_in_bf, batch, seq, win, tm):
    n, d = x.shape
    per_batch = seq // tm
    win_tiles = win // tm
    first_win = per_batch - win_tiles

    def cls(r):
        return pl.BlockSpec((r, tm // r, QKV_WIDTH),
                            lambda i: (i // per_batch, i % per_batch, 0))

    def kv_block(i):
        return ((i // per_batch) * win_tiles
                + jnp.maximum(i % per_batch - first_win, 0), 0)
    return pl.pallas_call(
        _proj_prompt_kernel,
        grid=(n // tm,),
        in_specs=[pl.BlockSpec((tm, d), lambda i: (i, 0)),
                  pl.BlockSpec((d, PROJ_WIDTH), lambda i: (0, 0))],
        out_specs=[pl.BlockSpec((tm, 2 * ATTN_WIDTH), kv_block),
                   pl.BlockSpec((tm, POOL_WIDTH), lambda i: (i, 0)),
                   pl.BlockSpec((tm, QKV_WIDTH), lambda i: (i, 0)),
                   cls(4), cls(16)],
        out_shape=[jax.ShapeDtypeStruct((batch * win, 2 * ATTN_WIDTH), _F32),
                   jax.ShapeDtypeStruct((n, POOL_WIDTH), _F32),
                   jax.ShapeDtypeStruct((n, QKV_WIDTH), _BF16),
                   jax.ShapeDtypeStruct((batch * 4, seq // 4, QKV_WIDTH), _BF16),
                   jax.ShapeDtypeStruct((batch * 16, seq // 16, QKV_WIDTH), _BF16)],
        scratch_shapes=[pltpu.VMEM((2, tm, LANES), _F32),
                        pltpu.VMEM((2, 4, tm // 4, LANES), _F32)],
        compiler_params=_params(1, 48),
        name="proj_prompt",
    )(x, w_in_bf)


def _attn_prompt_kernel(q1, q4, q16, k1, k1h, k4, k4h, k16, k16h,
                        v1, v1h, v4, v4h, v16, v16h, proj_s_ref, kt_ref, vt_ref,
                        out_ref, out_s_ref, o_scr, m_scr, l_scr, bias_scr, *, n_new, batch_s):
    step = ((pl.program_id(0) * pl.num_programs(1) + pl.program_id(1)) * pl.num_programs(2)
            + pl.program_id(2))

    @pl.when(step < batch_s)
    def _():
        _attn_sample_kernel(proj_s_ref, kt_ref, vt_ref, out_s_ref, n_new=n_new)

    blk = ATTN_BLOCK
    first_super = pl.program_id(2) == 0
    ii = lax.broadcasted_iota(jnp.int32, (blk, 2 * blk), 0)
    jj = lax.broadcasted_iota(jnp.int32, (blk, 2 * blk), 1)
    band = (jj >= ii) & (jj <= ii + WINDOW_STEPS)
    lo = jnp.where(first_super, blk, 0)
    bias_scr[0] = jnp.where(band, 0.0, NEG_INF)
    bias_scr[1] = jnp.where(band & (jj >= lo), 0.0, NEG_INF)
    head0 = lax.broadcasted_iota(jnp.int32, (blk, LANES), 1) < HEAD_DIM

    def block(g, r, q_ref, k_ref, kh_ref, v_ref, vh_ref, c, ib):
        off = ib * blk
        q = q_ref[c, off:off + blk, :]
        if ib == 0:
            bias = bias_scr[1]
            k2 = jnp.concatenate([kh_ref[c], k_ref[c, 0:blk, :]], axis=0)
            v2 = jnp.concatenate([vh_ref[c], v_ref[c, 0:blk, :]], axis=0)
        else:
            bias = bias_scr[0]
            k2 = k_ref[c, off - blk:off + blk, :]
            v2 = v_ref[c, off - blk:off + blk, :]
        zero = jnp.zeros_like(q)
        qq = jnp.concatenate([jnp.where(head0, q, zero), jnp.where(head0, zero, q)], axis=0)
        s = lax.dot_general(qq, k2, _NT, preferred_element_type=_F32)
        ms, ls, ps = [], [], []
        for h in range(HEADS_PER_TILE):
            sh = s[h * blk:(h + 1) * blk, :] + bias
            m = jnp.max(sh, axis=-1, keepdims=True)
            p = jnp.exp2(sh - m)
            ms.append(m)
            ls.append(jnp.sum(p, axis=-1, keepdims=True))
            ps.append(p.astype(_BF16))
        o = jnp.dot(jnp.concatenate(ps, axis=0), v2, preferred_element_type=_F32)
        if r == 1:
            rows = pl.ds(off, blk)
        else:
            rows = pl.ds(c + r * off, blk, stride=r)
        m_scr[g, rows, :] = jnp.where(head0, ms[0], ms[1])
        l_scr[g, rows, :] = jnp.where(head0, ls[0], ls[1])
        o_scr[g, rows, :] = jnp.where(head0, o[0:blk, :], o[blk:2 * blk, :])

    for g, (r, q_ref, k_ref, kh_ref, v_ref, vh_ref) in enumerate((
            (1, q1, k1, k1h, v1, v1h), (4, q4, k4, k4h, v4, v4h),
            (16, q16, k16, k16h, v16, v16h))):
        per_class = BLOCKS_PER_SUPER // r
        for c in range(r):
            for ib in range(per_class):
                block(g, r, q_ref, k_ref, kh_ref, v_ref, vh_ref, c, ib)

    chunk = 256

    def merge(ci, carry):
        rows = pl.ds(pl.multiple_of(ci * chunk, chunk), chunk)
        ms = [m_scr[g, rows, :] for g in range(3)]
        m = jnp.maximum(jnp.maximum(ms[0], ms[1]), ms[2])
        num = jnp.zeros((chunk, LANES), _F32)
        den = jnp.zeros((chunk, LANES), _F32)
        for g in range(3):
            a = jnp.exp2(ms[g] - m)
            num = num + a * o_scr[g, rows, :]
            den = den + a * l_scr[g, rows, :]
        out_ref[rows, :] = (num / den).astype(out_ref.dtype)
        return carry
    lax.fori_loop(0, SUPER_BLOCK // chunk, merge, 0)


def _attn_prompt(qkv, qkv4, qkv16, batch, seq, proj_s, cache_kt, cache_vt, layer, n_new):
    blk, sb = ATTN_BLOCK, SUPER_BLOCK
    n_super = seq // sb
    batch_s = proj_s.shape[0] // n_new
    assert batch * N_HEAD_TILES * n_super >= batch_s

    def sample_idx(b, hp, s):
        return jnp.minimum((b * N_HEAD_TILES + hp) * n_super + s, batch_s - 1)
    cache_spec = pl.BlockSpec((None, None, N_HEADS, HEAD_DIM, MAX_WINDOW),
                              lambda b, hp, s: (layer, sample_idx(b, hp, s), 0, 0, 0))
    kcol, vcol = N_HEAD_TILES, 2 * N_HEAD_TILES
    views = {1: qkv.reshape(batch, seq, QKV_WIDTH), 4: qkv4, 16: qkv16}

    def main(r, col):
        return pl.BlockSpec((r, sb // r, LANES), lambda b, hp, s: (b, s, col + hp))

    def halo(r, col):
        per = sb // (r * blk)
        return pl.BlockSpec((r, blk, LANES),
                            lambda b, hp, s: (b, jnp.maximum(s * per - 1, 0), col + hp))

    in_specs = [main(r, 0) for r in DILATIONS]
    operands = [views[r] for r in DILATIONS]
    for col in (kcol, vcol):
        for r in DILATIONS:
            in_specs += [main(r, col), halo(r, col)]
            operands += [views[r], views[r]]
    in_specs += [pl.BlockSpec((n_new, PROJ_WIDTH),
                              lambda b, hp, s: (sample_idx(b, hp, s), 0)),
                 cache_spec, cache_spec]
    operands += [proj_s, cache_kt, cache_vt]
    return pl.pallas_call(
        functools.partial(_attn_prompt_kernel, n_new=n_new, batch_s=batch_s),
        grid=(batch, N_HEAD_TILES, n_super),
        in_specs=in_specs,
        out_specs=[pl.BlockSpec((sb, LANES), lambda b, hp, s: (b * n_super + s, hp)),
                   pl.BlockSpec((n_new, ATTN_WIDTH),
                                lambda b, hp, s: (sample_idx(b, hp, s), 0))],
        out_shape=[jax.ShapeDtypeStruct((batch * seq, ATTN_WIDTH), _BF16),
                   jax.ShapeDtypeStruct((batch_s * n_new, ATTN_WIDTH), _F32)],
        scratch_shapes=[
            pltpu.VMEM((3, sb, LANES), _F32),
            pltpu.VMEM((3, sb, LANES), _F32),
            pltpu.VMEM((3, sb, LANES), _F32),
            pltpu.VMEM((2, blk, 2 * blk), _F32)],
        compiler_params=_params(3, 58),
        name="attn_prompt",
    )(*operands)


def _branch_count(dist):
    return ((dist <= 128).astype(_F32)
            + (((dist & 3) == 0) & (dist <= 512)).astype(_F32)
            + (((dist & 15) == 0) & (dist <= MAX_WINDOW)).astype(_F32))


def _attn_sample_kernel(proj_ref, kt_ref, vt_ref, out_ref, *, n_new):
    rows = HEADS_PER_TILE * n_new
    t_c = lax.broadcasted_iota(jnp.int32, (rows, MAX_WINDOW), 0) & (n_new - 1)
    f_c = lax.broadcasted_iota(jnp.int32, (rows, MAX_WINDOW), 1)
    w_c = _branch_count(MAX_WINDOW + t_c - f_c)
    t_n = lax.broadcasted_iota(jnp.int32, (rows, LANES), 0) & (n_new - 1)
    f_n = lax.broadcasted_iota(jnp.int32, (rows, LANES), 1)
    d_n = t_n - f_n
    w_n = jnp.where((d_n >= 0) & (f_n < n_new), _branch_count(d_n), 0.0)
    head0 = lax.broadcasted_iota(jnp.int32, (n_new, LANES), 1) < HEAD_DIM
    pad = jnp.zeros((LANES - n_new, LANES), _F32)

    for hp in range(N_HEAD_TILES):
        sl = slice(hp * LANES, (hp + 1) * LANES)
        ksl = slice(ATTN_WIDTH + hp * LANES, ATTN_WIDTH + (hp + 1) * LANES)
        vsl = slice(2 * ATTN_WIDTH + hp * LANES, 2 * ATTN_WIDTH + (hp + 1) * LANES)
        heads = slice(hp * HEADS_PER_TILE, (hp + 1) * HEADS_PER_TILE)
        q = proj_ref[:, sl] * QK_SCALE
        q2 = jnp.concatenate([jnp.where(head0, q, 0.0),
                              jnp.where(head0, 0.0, q)], axis=0).astype(_BF16)
        kt = kt_ref[heads].reshape(LANES, MAX_WINDOW).astype(_BF16)
        vt = vt_ref[heads].reshape(LANES, MAX_WINDOW).astype(_BF16)
        k_n = jnp.concatenate([proj_ref[:, ksl], pad], axis=0).astype(_BF16)
        v_n = jnp.concatenate([proj_ref[:, vsl], pad], axis=0).astype(_BF16)
        s_c = jnp.dot(q2, kt, preferred_element_type=_F32)
        s_n = lax.dot_general(q2, k_n, _NT, preferred_element_type=_F32)
        s_c = jnp.where(w_c > 0.0, s_c, NEG_INF)
        s_n = jnp.where(w_n > 0.0, s_n, NEG_INF)
        m = jnp.maximum(jnp.max(s_c, axis=-1, keepdims=True),
                        jnp.max(s_n, axis=-1, keepdims=True))
        p_c = w_c * jnp.exp(s_c - m)
        p_n = w_n * jnp.exp(s_n - m)
        l = jnp.sum(p_c, axis=-1, keepdims=True) + jnp.sum(p_n, axis=-1, keepdims=True)
        o = (lax.dot_general(p_c.astype(_BF16), vt, _NT, preferred_element_type=_F32)
             + jnp.dot(p_n.astype(_BF16), v_n, preferred_element_type=_F32)) / l
        out_ref[:, sl] = jnp.where(head0, o[0:n_new, :], o[n_new:rows, :])


def _pool_rows(ext_ref, base, pos_first, w_ref, b_ref, sc_ref, bb, chunk):
    n = bb * chunk
    lane = lax.broadcasted_iota(jnp.int32, (n, POOL_WIDTH), 1)
    window = jnp.where(lane < POOL_GROUP, 2,
                       jnp.where(lane < 2 * POOL_GROUP, 4,
                                 jnp.where(lane < 3 * POOL_GROUP, 8, 16)))
    row = lax.broadcasted_iota(jnp.int32, (bb, chunk, POOL_WIDTH), 1).reshape(n, POOL_WIDTH)

    def shifted(i):
        lo = base + POOL_HALO - i
        return ext_ref[:, lo:lo + chunk, :].reshape(n, POOL_WIDTH)
    cur = shifted(0)
    acc = cur + shifted(1)
    total = acc
    for g, w in enumerate(POOL_WINDOWS[1:], start=1):
        for i in range(w // 2, w):
            acc = acc + shifted(i)
        total = jnp.where(lane >= g * POOL_GROUP, acc, total)
    count = jnp.minimum(pos_first + row + 1, window).astype(_F32)
    diff = total / count - cur
    y = jnp.dot(diff.astype(_BF16), w_ref[...], preferred_element_type=_F32)
    return (y + b_ref[...]) * sc_ref[...]


def _pool_kernel(ext_ref, w_ref, b_ref, sc_ref, out_ref, *, bb, chunk, pos0):
    y = _pool_rows(ext_ref, 0, pos0, w_ref, b_ref, sc_ref, bb, chunk)
    out_ref[...] = y.astype(out_ref.dtype)


def _pool(ext, w_bd, bias, scale, pos0):
    batch, rows, _ = ext.shape
    seq = rows - POOL_HALO
    return pl.pallas_call(
        functools.partial(_pool_kernel, bb=batch, chunk=seq, pos0=pos0),
        grid=(1,),
        in_specs=[pl.BlockSpec((batch, rows, POOL_WIDTH), lambda i: (0, 0, 0)),
                  pl.BlockSpec((POOL_WIDTH, POOL_WIDTH), lambda i: (0, 0)),
                  pl.BlockSpec((1, POOL_WIDTH), lambda i: (0, 0)),
                  pl.BlockSpec((1, POOL_WIDTH), lambda i: (0, 0))],
        out_specs=pl.BlockSpec((batch * seq, POOL_WIDTH), lambda i: (0, 0)),
        out_shape=jax.ShapeDtypeStruct((batch * seq, POOL_WIDTH), _BF16),
        compiler_params=_params(1, 48),
        name="pool",
    )(ext, w_bd, bias, scale)


def _mix_kernel(a_ref, p_ref, x_ref, w_ref, g_ref, b_ref, out_ref, *, alpha):
    mix = (jnp.dot(a_ref[...], w_ref[0:ATTN_WIDTH, :], preferred_element_type=_F32)
           + jnp.dot(p_ref[...], w_ref[ATTN_WIDTH:, :], preferred_element_type=_F32))
    out_ref[...] = _layer_norm(alpha * x_ref[...] + mix, g_ref[...], b_ref[...])


def _mix(a, p, x, w_out_bf, g, b, alpha, tm):
    n, d = x.shape
    return pl.pallas_call(
        functools.partial(_mix_kernel, alpha=alpha),
        grid=(n // tm,),
        in_specs=[pl.BlockSpec((tm, ATTN_WIDTH), lambda i: (i, 0)),
                  pl.BlockSpec((tm, POOL_WIDTH), lambda i: (i, 0)),
                  pl.BlockSpec((tm, d), lambda i: (i, 0)),
                  pl.BlockSpec((ATTN_WIDTH + POOL_WIDTH, d), lambda i: (0, 0)),
                  pl.BlockSpec((1, d), lambda i: (0, 0)),
                  pl.BlockSpec((1, d), lambda i: (0, 0))],
        out_specs=pl.BlockSpec((tm, d), lambda i: (i, 0)),
        out_shape=jax.ShapeDtypeStruct((n, d), _F32),
        compiler_params=_params(1, 48),
        name="mix",
    )(a, p, x, w_out_bf, g, b)


_POOL_CHUNK = 128


def _mix_prompt_kernel(a_ref, u_ref, uh_ref, x_ref, w_ref, wp_ref, bp_ref, sp_ref,
                       g_ref, b_ref, out_ref, ext_scr, p_scr, *, alpha, tiles_per_seq):
    tm = x_ref.shape[0]
    ti = pl.program_id(0) % tiles_per_seq
    ext_scr[0, 0:POOL_HALO, :] = jnp.where(ti == 0, 0.0, uh_ref[...])
    ext_scr[0, POOL_HALO:, :] = u_ref[...]
    for r0 in range(0, tm, _POOL_CHUNK):
        y = _pool_rows(ext_scr, r0, ti * tm + r0, wp_ref, bp_ref, sp_ref, 1, _POOL_CHUNK)
        p_scr[r0:r0 + _POOL_CHUNK, :] = y.astype(_BF16)
    mix = (jnp.dot(a_ref[...], w_ref[0:ATTN_WIDTH, :], preferred_element_type=_F32)
           + jnp.dot(p_scr[...], w_ref[ATTN_WIDTH:, :], preferred_element_type=_F32))
    out_ref[...] = _layer_norm(alpha * x_ref[...] + mix, g_ref[...], b_ref[...])


def _mix_prompt(a, u, x, w_out_bf, w_bd, b_pool, s_pool, g, b, alpha, seq, tm):
    n, d = x.shape
    assert seq % tm == 0 and tm % _POOL_CHUNK == 0
    halo_per_tile = tm // POOL_HALO
    return pl.pallas_call(
        functools.partial(_mix_prompt_kernel, alpha=alpha, tiles_per_seq=seq // tm),
        grid=(n // tm,),
        in_specs=[pl.BlockSpec((tm, ATTN_WIDTH), lambda i: (i, 0)),
                  pl.BlockSpec((tm, POOL_WIDTH), lambda i: (i, 0)),
                  pl.BlockSpec((POOL_HALO, POOL_WIDTH),
                               lambda i: (jnp.maximum(i * halo_per_tile - 1, 0), 0)),
                  pl.BlockSpec((tm, d), lambda i: (i, 0)),
                  pl.BlockSpec((ATTN_WIDTH + POOL_WIDTH, d), lambda i: (0, 0)),
                  pl.BlockSpec((POOL_WIDTH, POOL_WIDTH), lambda i: (0, 0)),
                  pl.BlockSpec((1, POOL_WIDTH), lambda i: (0, 0)),
                  pl.BlockSpec((1, POOL_WIDTH), lambda i: (0, 0)),
                  pl.BlockSpec((1, d), lambda i: (0, 0)),
                  pl.BlockSpec((1, d), lambda i: (0, 0))],
        out_specs=pl.BlockSpec((tm, d), lambda i: (i, 0)),
        out_shape=jax.ShapeDtypeStruct((n, d), _F32),
        scratch_shapes=[pltpu.VMEM((1, tm + POOL_HALO, POOL_WIDTH), _F32),
                        pltpu.VMEM((tm, POOL_WIDTH), _BF16)],
        compiler_params=_params(1, 48),
        name="mix_prompt",
    )(a, u, u, x, w_out_bf, w_bd, b_pool, s_pool, g, b)


_ROUTER_LANES = LANES
_EXPERT_LANE0 = N_EXPERT_GROUPS
_MOE_ROW_CHUNK = 256


def _router_combine(h, hb, wr_ref, br_ref):
    h_lo = (h - hb.astype(_F32)).astype(_BF16)
    hi_pass = jnp.dot(hb, wr_ref[...], preferred_element_type=_F32)
    lo_pass = jnp.dot(h_lo, wr_ref[:, 0:_ROUTER_LANES], preferred_element_type=_F32)
    logits = (hi_pass[:, 0:_ROUTER_LANES] + hi_pass[:, _ROUTER_LANES:]
              + lo_pass + br_ref[...])
    lane = lax.broadcasted_iota(jnp.int32, logits.shape, 1)
    big = _ROUTER_LANES
    g_logit = jnp.where(lane < N_EXPERT_GROUPS, logits, NEG_INF)
    g_max = jnp.max(g_logit, axis=-1, keepdims=True)
    g_sel = jnp.min(jnp.where(g_logit == g_max, lane, big), axis=-1, keepdims=True)
    p_group = 1.0 / jnp.sum(jnp.exp(g_logit - g_max), axis=-1, keepdims=True)
    lane_group = (lane - _EXPERT_LANE0) >> 2
    in_group = ((lane >= _EXPERT_LANE0) & (lane < _EXPERT_LANE0 + N_EXPERTS)
                & (lane_group == g_sel))
    e_logit = jnp.where(in_group, logits, NEG_INF)
    v1 = jnp.max(e_logit, axis=-1, keepdims=True)
    i1 = jnp.min(jnp.where(e_logit == v1, lane, big), axis=-1, keepdims=True)
    e_rest = jnp.where(lane == i1, NEG_INF, e_logit)
    v2 = jnp.max(e_rest, axis=-1, keepdims=True)
    i2 = jnp.min(jnp.where(e_rest == v2, lane, big), axis=-1, keepdims=True)
    t = jnp.exp(v2 - v1)
    w1 = 1.0 / (1.0 + t)
    w2 = t / (1.0 + t)
    return p_group * (jnp.where(lane == i1, w1, 0.0) + jnp.where(lane == i2, w2, 0.0))


def _moe_kernel(h_ref, hs_ref, wr_ref, br_ref, wg_ref, wu_ref, wd_ref, g_ref, b_ref,
                out_ref, outs_ref, hb_scr, comb_scr, hbs_scr, combs_scr, *, alpha):
    e = pl.program_id(1)
    rider = pl.program_id(0) == 0

    def route(src_ref, hb_dst, comb_dst, acc_ref):
        rows = src_ref.shape[0]
        chunk = min(rows, _MOE_ROW_CHUNK)
        for r0 in range(0, rows, chunk):
            h = src_ref[r0:r0 + chunk, :]
            hb = h.astype(_BF16)
            hb_dst[r0:r0 + chunk, :] = hb
            comb_dst[r0:r0 + chunk, :] = _router_combine(h, hb, wr_ref, br_ref)
        acc_ref[...] = jnp.zeros_like(acc_ref)

    def expert(hb_src, comb_src, acc_ref):
        comb = comb_src[...]
        lane = lax.broadcasted_iota(jnp.int32, comb.shape, 1)
        gate = jnp.sum(jnp.where(lane == _EXPERT_LANE0 + e, comb, 0.0), axis=-1,
                       keepdims=True)
        hb = hb_src[...]
        hg = jnp.dot(hb, wg_ref[...].astype(_BF16), preferred_element_type=_F32)
        hu = jnp.dot(hb, wu_ref[...].astype(_BF16), preferred_element_type=_F32)
        act = hg * (1.0 / (1.0 + jnp.exp(-hg))) * hu * gate
        acc_ref[...] += jnp.dot(act.astype(_BF16), wd_ref[...].astype(_BF16),
                                preferred_element_type=_F32)

    def finish(src_ref, acc_ref):
        acc_ref[...] = _layer_norm(alpha * src_ref[...] + acc_ref[...], g_ref[...], b_ref[...])

    pl.when(e == 0)(lambda: route(h_ref, hb_scr, comb_scr, out_ref))
    pl.when(rider & (e == 0))(lambda: route(hs_ref, hbs_scr, combs_scr, outs_ref))
    expert(hb_scr, comb_scr, out_ref)
    pl.when(rider)(lambda: expert(hbs_scr, combs_scr, outs_ref))
    pl.when(e == N_EXPERTS - 1)(lambda: finish(h_ref, out_ref))
    pl.when(rider & (e == N_EXPERTS - 1))(lambda: finish(hs_ref, outs_ref))


def _moe(h, hs, w_router, b_router, w_gate, w_up, w_down, layer, g, b, alpha, tm):
    n, d = h.shape
    ns = hs.shape[0]
    return pl.pallas_call(
        functools.partial(_moe_kernel, alpha=alpha),
        grid=(n // tm, N_EXPERTS),
        in_specs=[pl.BlockSpec((tm, d), lambda i, e: (i, 0)),
                  pl.BlockSpec((ns, d), lambda i, e: (0, 0), pipeline_mode=pl.Buffered(1)),
                  pl.BlockSpec((d, 2 * _ROUTER_LANES), lambda i, e: (0, 0)),
                  pl.BlockSpec((1, _ROUTER_LANES), lambda i, e: (0, 0)),
                  pl.BlockSpec((None, None, d, D_EXPERT), lambda i, e: (layer, e, 0, 0)),
                  pl.BlockSpec((None, None, d, D_EXPERT), lambda i, e: (layer, e, 0, 0)),
                  pl.BlockSpec((None, None, D_EXPERT, d), lambda i, e: (layer, e, 0, 0)),
                  pl.BlockSpec((1, d), lambda i, e: (0, 0)),
                  pl.BlockSpec((1, d), lambda i, e: (0, 0))],
        out_specs=[pl.BlockSpec((tm, d), lambda i, e: (i, 0)),
                   pl.BlockSpec((ns, d), lambda i, e: (0, 0))],
        out_shape=[jax.ShapeDtypeStruct((n, d), _F32),
                   jax.ShapeDtypeStruct((ns, d), _F32)],
        scratch_shapes=[pltpu.VMEM((tm, d), _BF16),
                        pltpu.VMEM((tm, _ROUTER_LANES), _F32),
                        pltpu.VMEM((ns, d), _BF16),
                        pltpu.VMEM((ns, _ROUTER_LANES), _F32)],
        compiler_params=_params(2, 63),
        name="moe",
    )(h, hs, w_router, b_router, w_gate, w_up, w_down, g, b)


def _block_diag(w_pool):
    n = len(POOL_WINDOWS)
    eye = jnp.eye(n, dtype=w_pool.dtype)
    return jnp.einsum('gce,gh->gche', w_pool, eye).reshape(POOL_WIDTH, POOL_WIDTH)


def _router_weights(w_rg, b_rg, w_re, b_re):
    d = w_rg.shape[0]
    w_e = jnp.transpose(w_re, (1, 0, 2)).reshape(d, N_EXPERTS)
    w = jnp.concatenate([w_rg, w_e], axis=1)
    w = jnp.pad(w, ((0, 0), (0, _ROUTER_LANES - w.shape[1])))
    bias = jnp.concatenate([b_rg, b_re.reshape(N_EXPERTS)])
    bias = jnp.pad(bias, (0, _ROUTER_LANES - bias.shape[0])).reshape(1, _ROUTER_LANES)
    w_hi = w.astype(_BF16)
    w_lo = (w - w_hi.astype(_F32)).astype(_BF16)
    return jnp.concatenate([w_hi, w_lo], axis=1), bias


def _row_tile(n, target):
    return target if n % target == 0 else n


def kernel(x_prompt, x_sample, cache_k, cache_v, state_pool, w_in, w_out, w_pool, b_pool,
           pool_scale, ln1_g, ln1_b, w_rg, b_rg, w_re, b_re, w_gate, w_up, w_down, ln2_g, ln2_b):
    depth = w_in.shape[0]
    alpha = float((2 * depth) ** 0.25)
    bp, tp, d = x_prompt.shape
    bs, ts, _ = x_sample.shape
    kv_rows = cache_k.shape[2]
    assert kv_rows == MAX_WINDOW and tp % SUPER_BLOCK == 0 and ts == SUBLANES
    win_p = min(MAX_WINDOW, tp)

    xp = x_prompt.reshape(bp * tp, d)
    xs = x_sample.reshape(bs * ts, d)
    ckt = jnp.transpose(cache_k, (0, 1, 3, 4, 2))
    cvt = jnp.transpose(cache_v, (0, 1, 3, 4, 2))

    pk, pv, pu, sk, sv, su = [], [], [], [], [], []
    for l in range(depth):
        w_in_bf = w_in[l].astype(_BF16)
        w_out_bf = w_out[l].astype(_BF16)
        w_bd = _block_diag(w_pool[l]).astype(_BF16)
        b_pl = b_pool[l].reshape(1, POOL_WIDTH)
        sc_pl = pool_scale[l].reshape(1, POOL_WIDTH)
        w_router, b_router = _router_weights(w_rg[l], b_rg[l], w_re[l], b_re[l])
        g1, b1 = ln1_g[l].reshape(1, d), ln1_b[l].reshape(1, d)
        g2, b2 = ln2_g[l].reshape(1, d), ln2_b[l].reshape(1, d)

        kv_p, u_p, qkv_p, qkv4_p, qkv16_p = _proj_prompt(xp, w_in_bf, bp, tp, win_p, 512)
        proj_s = _proj_sample(xs, w_in_bf)
        a_p, a_s = _attn_prompt(qkv_p, qkv4_p, qkv16_p, bp, tp, proj_s, ckt, cvt, l, ts)
        h_p = _mix_prompt(a_p, u_p, xp, w_out_bf, w_bd, b_pl, sc_pl, g1, b1, alpha, tp, 512)
        kv_p3 = kv_p.reshape(bp, win_p, 2 * ATTN_WIDTH)
        pk.append(kv_p3[:, :, 0:ATTN_WIDTH].reshape(bp, win_p, N_HEADS, HEAD_DIM))
        pv.append(kv_p3[:, :, ATTN_WIDTH:].reshape(bp, win_p, N_HEADS, HEAD_DIM))
        pu.append(u_p.reshape(bp, tp, POOL_WIDTH)[:, tp - POOL_PREFIX:])

        proj_s3 = proj_s.reshape(bs, ts, PROJ_WIDTH)
        u_s = proj_s3[:, :, QKV_WIDTH:]
        ext_s = jnp.concatenate(
            [jnp.zeros((bs, POOL_HALO - POOL_PREFIX, POOL_WIDTH), _F32), state_pool[l], u_s],
            axis=1)
        p_s = _pool(ext_s, w_bd, b_pl, sc_pl, PAST_LEN)
        h_s = _mix(a_s.astype(_BF16), p_s, xs, w_out_bf, g1, b1, alpha, bs * ts)
        xp, xs = _moe(h_p, h_s, w_router, b_router, w_gate, w_up, w_down, l, g2, b2, alpha,
                      _row_tile(bp * tp, 2048))
        sk.append(proj_s3[:, :, ATTN_WIDTH:2 * ATTN_WIDTH].reshape(bs, ts, N_HEADS, HEAD_DIM))
        sv.append(proj_s3[:, :, 2 * ATTN_WIDTH:QKV_WIDTH].reshape(bs, ts, N_HEADS, HEAD_DIM))
        su.append(u_s)

    return (xp.reshape(bp, tp, d), xs.reshape(bs, ts, d), jnp.stack(pk), jnp.stack(pv),
            jnp.stack(pu), jnp.stack(sk), jnp.stack(sv), jnp.stack(su))
```
